```python
import math
import jax
import jax.numpy as jnp
from jax import lax
import numpy as np

D_MODEL = 2048
BATCH = 8
SEQ = 2048
DEPTH = 2

GRID_W = 64
CTX_LEN = 256
EPS = 1e-6

A_WIDTH = D_MODEL // 2
A_GROUP_DIM = 128
A_GROUPS = A_WIDTH // A_GROUP_DIM
CHUNK = 128
B_WIDTH = D_MODEL // 2
B_GROUP_DIM = 128
B_GROUPS = B_WIDTH // B_GROUP_DIM
AB_IN = 2 * A_WIDTH + B_WIDTH
AB_OUT = A_WIDTH + B_WIDTH

D_INNER = 2 * D_MODEL
HEADDIM = 64
N_SSD_HEADS = D_INNER // HEADDIM
D_STATE = 128
N_BC_GROUPS = 8
HEADS_PER_GROUP = N_SSD_HEADS // N_BC_GROUPS
D_CONV = 5
SSD_CHUNK = 128
GN = N_BC_GROUPS * D_STATE
CONV_DIM = D_INNER + 2 * GN
SSD_IN = D_INNER + CONV_DIM + 2 * N_SSD_HEADS

N_EXPERTS = 16
D_EXPERT = D_MODEL
CAPACITY_FACTOR = 2

N_EVEN = (DEPTH + 1) // 2
N_ODD = DEPTH // 2

kernel_name = "hybrid_gmlp_fnet_ssd_ecmoe_dit"

F32 = jnp.float32


def rmsnorm(x, g):
    xf = x.astype(F32)
    y = xf * lax.rsqrt(jnp.mean(xf * xf, axis=-1, keepdims=True) + EPS)
    return (y * g.astype(F32)).astype(x.dtype)


def modulate(x, g, shift, scale):
    return rmsnorm(x, g) * (1 + scale) + shift


def adaln(cond, w, b):
    m = jnp.dot(jax.nn.silu(cond), w) + b
    return jnp.split(m[..., None, :], 6, axis=-1)


def sincos_2d(rows, cols, dim):
    quarter = dim // 4
    omega = 1.0 / (10000.0 ** (jnp.arange(quarter, dtype=F32) / quarter))
    r = jnp.repeat(jnp.arange(rows, dtype=F32), cols)[:, None] * omega
    cl = jnp.tile(jnp.arange(cols, dtype=F32), rows)[:, None] * omega
    return jnp.concatenate([jnp.sin(r), jnp.cos(r), jnp.sin(cl), jnp.cos(cl)], axis=-1)


def chunk_gmlp(u, v, v_g, w_s, b_s):
    bsz, t, _ = v.shape
    vn = rmsnorm(v, v_g).reshape(bsz, t // CHUNK, CHUNK, A_GROUPS, A_GROUP_DIM)
    s = jnp.einsum('gij,bcjgd->bcigd', w_s, vn) + b_s.T[:, :, None]
    return u * s.reshape(bsz, t, A_WIDTH)


def fourier_mix(z):
    bsz, t, _ = z.shape
    zg = z.astype(F32).reshape(bsz, t, B_GROUPS, B_GROUP_DIM)
    f = jnp.fft.fft2(zg, axes=(1, 3), norm='ortho').real
    return f.reshape(bsz, t, B_WIDTH).astype(z.dtype)


def mixer_ab(h, w_in, w_out, v_g, w_s, b_s):
    p = jnp.dot(h, w_in)
    a = jax.nn.gelu(p[..., :2 * A_WIDTH], approximate=False)
    ya = chunk_gmlp(a[..., :A_WIDTH], a[..., A_WIDTH:], v_g, w_s, b_s)
    yb = fourier_mix(p[..., 2 * A_WIDTH:])
    return jnp.dot(jnp.concatenate([ya, yb], axis=-1), w_out)


def dwconv(z, w, b):
    ch = z.shape[-1]
    y = lax.conv_general_dilated(z, w[:, None, :].astype(z.dtype), window_strides=(1,),
                                 padding=[(D_CONV // 2, D_CONV // 2)],
                                 dimension_numbers=('NWC', 'WIO', 'NWC'), feature_group_count=ch)
    return jax.nn.silu(y + b)


def ssd_scan(x, dt, a, bm, cm, s0):
    bsz, t = x.shape[:2]
    nc, L = t // SSD_CHUNK, SSD_CHUNK
    xg = (x * dt[..., None]).reshape(bsz, nc, L, N_BC_GROUPS, HEADS_PER_GROUP, HEADDIM)
    cum = jnp.cumsum((dt * a).reshape(bsz, nc, L, N_BC_GROUPS, HEADS_PER_GROUP), axis=2)
    bm = bm.reshape(bsz, nc, L, N_BC_GROUPS, D_STATE)
    cm = cm.reshape(bsz, nc, L, N_BC_GROUPS, D_STATE)
    mask = jnp.tril(jnp.ones((L, L), dtype=bool))[:, :, None, None]
    seg = cum[:, :, :, None] - cum[:, :, None, :]
    decay = jnp.exp(jnp.where(mask, seg, -jnp.inf))
    cb = jnp.einsum('bcign,bcjgn->bcijg', cm, bm)
    y_intra = jnp.einsum('bcijg,bcijgk,bcjgkp->bcigkp', cb, decay, xg)
    decay_end = jnp.exp(cum[:, :, -1:] - cum)
    states = jnp.einsum('bclgn,bclgk,bclgkp->bcgkpn', bm, decay_end, xg)
    chunk_decay = jnp.exp(cum[:, :, -1])

    def step(s, inp):
        st, dec = inp
        return s * dec[..., None, None] + st, s

    s_fin, s_starts = lax.scan(step, s0, (jnp.moveaxis(states, 1, 0), jnp.moveaxis(chunk_decay, 1, 0)))
    s_starts = jnp.moveaxis(s_starts, 0, 1)
    y_inter = jnp.einsum('bclgn,bclgk,bcgkpn->bclgkp', cm, jnp.exp(cum), s_starts)
    return (y_intra + y_inter).reshape(bsz, t, N_SSD_HEADS, HEADDIM), s_fin


def ssd_final_state(x, dt, a, bm):
    bsz, t = x.shape[:2]
    cum = jnp.cumsum(dt * a, axis=1)
    w = jnp.exp(cum[:, -1:] - cum) * dt
    xg = (x * w[..., None]).reshape(bsz, t, N_BC_GROUPS, HEADS_PER_GROUP, HEADDIM)
    return jnp.einsum('btgn,btgkp->bgkpn', bm, xg)


def ssd_mix(h_lat, h_ctx, w_in, conv_w, conv_b, dt_bias, a_log, d_skip, norm_g, w_out, need_ctx_out):
    a = -jnp.exp(a_log.astype(F32))
    flip = lambda q: jnp.flip(q, axis=1)

    def to_dt(raw):
        b_, t_ = raw.shape[:2]
        return jax.nn.softplus(raw.astype(F32).reshape(b_, t_, 2, N_SSD_HEADS) + dt_bias.astype(F32))

    def full(h, s_init):
        b_, t_ = h.shape[:2]
        p = jnp.dot(h, w_in)
        z = p[..., :D_INNER]
        xbc = dwconv(p[..., D_INNER:D_INNER + CONV_DIM], conv_w, conv_b).astype(F32)
        dt = to_dt(p[..., D_INNER + CONV_DIM:])
        xs = xbc[..., :D_INNER].reshape(b_, t_, N_SSD_HEADS, HEADDIM)
        bm = xbc[..., D_INNER:D_INNER + GN].reshape(b_, t_, N_BC_GROUPS, D_STATE)
        cm = xbc[..., D_INNER + GN:].reshape(b_, t_, N_BC_GROUPS, D_STATE)
        y_f, s_f = ssd_scan(xs, dt[:, :, 0], a[0], bm, cm, s_init[0])
        y_b, s_b = ssd_scan(flip(xs), flip(dt[:, :, 1]), a[1], flip(bm), flip(cm), s_init[1])
        y = y_f + flip(y_b) + xs * d_skip.astype(F32)[:, None]
        y = y.reshape(b_, t_, D_INNER) * jax.nn.silu(z.astype(F32))
        return jnp.dot(rmsnorm(y, norm_g).astype(h.dtype), w_out), (s_f, s_b)

    def states_only(h):
        b_, t_ = h.shape[:2]
        w_sel = jnp.concatenate([w_in[:, D_INNER:2 * D_INNER + GN], w_in[:, D_INNER + CONV_DIM:]], axis=1)
        p = jnp.dot(h, w_sel)
        xb = dwconv(p[..., :D_INNER + GN], conv_w[:, :D_INNER + GN], conv_b[:D_INNER + GN]).astype(F32)
        dt = to_dt(p[..., D_INNER + GN:])
        xs = xb[..., :D_INNER].reshape(b_, t_, N_SSD_HEADS, HEADDIM)
        bm = xb[..., D_INNER:].reshape(b_, t_, N_BC_GROUPS, D_STATE)
        s_f = ssd_final_state(xs, dt[:, :, 0], a[0], bm)
        s_b = ssd_final_state(flip(xs), flip(dt[:, :, 1]), a[1], flip(bm))
        return (s_f, s_b)

    if need_ctx_out:
        zero = jnp.zeros((h_ctx.shape[0], N_BC_GROUPS, HEADS_PER_GROUP, HEADDIM, D_STATE), F32)
        y_ctx, s_ctx = full(h_ctx, (zero, zero))
    else:
        y_ctx, s_ctx = None, states_only(h_ctx)
    y_lat, _ = full(h_lat, s_ctx)
    return y_lat, y_ctx


def expert_choice_ffn(h, w_router, w_gate, w_up, w_down):
    bsz, t, _ = h.shape
    cap = CAPACITY_FACTOR * t // N_EXPERTS
    probs = jax.nn.softmax(jnp.dot(h, w_router).astype(F32), axis=-1)
    gate, idx = lax.top_k(jnp.swapaxes(probs, 1, 2), cap)
    bidx = jnp.arange(bsz)[:, None, None]
    xe = h[bidx, idx]
    hid = jax.nn.silu(jnp.einsum('becd,edf->becf', xe, w_gate)) * jnp.einsum('becd,edf->becf', xe, w_up)
    ye = jnp.einsum('becf,efd->becd', hid, w_down) * gate[..., None].astype(h.dtype)
    return jnp.zeros_like(h).at[bidx, idx].add(ye)


def setup_inputs(seed: int = 0):
    key = jax.random.key(seed)
    k = jax.random.split(key, 26)
    nrm = lambda i, shape, s: jax.random.normal(k[i], shape, F32) * s
    dt0 = jnp.exp(jax.random.uniform(k[16], (N_ODD, 2, N_SSD_HEADS), F32, math.log(1e-3), math.log(1e-1)))
    return {
        'x': nrm(0, (BATCH, SEQ, D_MODEL), 1.0),
        'c': nrm(1, (BATCH, D_MODEL), 1.0),
        'ctx': nrm(2, (BATCH, CTX_LEN, D_MODEL), 1.0),
        'c_ctx': nrm(3, (D_MODEL,), 1.0),
        'mod_w': nrm(4, (DEPTH, D_MODEL, 6 * D_MODEL), 0.5 * D_MODEL ** -0.5),
        'mod_b': nrm(5, (DEPTH, 6 * D_MODEL), 0.01),
        'norm_g': 1.0 + nrm(6, (DEPTH, 2, D_MODEL), 0.02),
        'final_g': 1.0 + nrm(7, (D_MODEL,), 0.02),
        'ab_w_in': nrm(8, (N_EVEN, D_MODEL, AB_IN), D_MODEL ** -0.5),
        'ab_w_out': nrm(9, (N_EVEN, AB_OUT, D_MODEL), AB_OUT ** -0.5),
        'gm_v_g': 1.0 + nrm(10, (N_EVEN, A_WIDTH), 0.02),
        'gm_w_s': nrm(11, (N_EVEN, A_GROUPS, CHUNK, CHUNK), CHUNK ** -0.5),
        'gm_b_s': 1.0 + nrm(12, (N_EVEN, A_GROUPS, CHUNK), 0.01),
        'ssd_w_in': nrm(13, (N_ODD, D_MODEL, SSD_IN), D_MODEL ** -0.5),
        'ssd_conv_w': nrm(14, (N_ODD, D_CONV, CONV_DIM), D_CONV ** -0.5),
        'ssd_conv_b': nrm(15, (N_ODD, CONV_DIM), 0.01),
        'ssd_dt_bias': dt0 + jnp.log(-jnp.expm1(-dt0)),
        'ssd_a_log': jnp.log(jax.random.uniform(k[17], (N_ODD, 2, N_SSD_HEADS), F32, 1.0, 16.0)),
        'ssd_d': 1.0 + nrm(18, (N_ODD, N_SSD_HEADS), 0.02),
        'ssd_norm_g': 1.0 + nrm(19, (N_ODD, D_INNER), 0.02),
        'ssd_w_out': nrm(20, (N_ODD, D_INNER, D_MODEL), D_INNER ** -0.5),
        'moe_w_router': nrm(21, (DEPTH, D_MODEL, N_EXPERTS), D_MODEL ** -0.5),
        'moe_w_gate': nrm(22, (DEPTH, N_EXPERTS, D_MODEL, D_EXPERT), D_MODEL ** -0.5),
        'moe_w_up': nrm(23, (DEPTH, N_EXPERTS, D_MODEL, D_EXPERT), D_MODEL ** -0.5),
        'moe_w_down': nrm(24, (DEPTH, N_EXPERTS, D_EXPERT, D_MODEL), D_EXPERT ** -0.5),
    }


def reference(x, c, ctx, c_ctx, mod_w, mod_b, norm_g, final_g, ab_w_in, ab_w_out, gm_v_g, gm_w_s, gm_b_s,
              ssd_w_in, ssd_conv_w, ssd_conv_b, ssd_dt_bias, ssd_a_log, ssd_d, ssd_norm_g, ssd_w_out,
              moe_w_router, moe_w_gate, moe_w_up, moe_w_down):
    n = x.shape[1]
    rows = n // GRID_W
    h = x + sincos_2d(rows, GRID_W, D_MODEL).astype(x.dtype)[None]
    hc = ctx
    for i in range(DEPTH):
        last = i == DEPTH - 1
        even = i % 2 == 0
        j = i // 2
        ctx_reaches_latent = (not last) or (not even)
        sh1, sc1, g1, sh2, sc2, g2 = adaln(c, mod_w[i], mod_b[i])
        xn = modulate(h, norm_g[i, 0], sh1, sc1)
        if ctx_reaches_latent:
            csh1, csc1, cg1, csh2, csc2, cg2 = adaln(c_ctx, mod_w[i], mod_b[i])
            xc = modulate(hc, norm_g[i, 0], csh1, csc1)
        if even:
            y = mixer_ab(xn, ab_w_in[j], ab_w_out[j], gm_v_g[j], gm_w_s[j], gm_b_s[j])
            if not last:
                yc = mixer_ab(xc, ab_w_in[j], ab_w_out[j], gm_v_g[j], gm_w_s[j], gm_b_s[j])
        else:
            y, yc = ssd_mix(xn, xc, ssd_w_in[j], ssd_conv_w[j], ssd_conv_b[j], ssd_dt_bias[j], ssd_a_log[j],
                            ssd_d[j], ssd_norm_g[j], ssd_w_out[j], not last)
        h = h + g1 * y
        h = h + g2 * expert_choice_ffn(modulate(h, norm_g[i, 1], sh2, sc2),
                                       moe_w_router[i], moe_w_gate[i], moe_w_up[i], moe_w_down[i])
        if not last:
            hc = hc + cg1 * yc
            hc = hc + cg2 * expert_choice_ffn(modulate(hc, norm_g[i, 1], csh2, csc2),
                                              moe_w_router[i], moe_w_gate[i], moe_w_up[i], moe_w_down[i])
    return rmsnorm(h, final_g)
```

```python
import functools
import math

import jax
import jax.numpy as jnp
import numpy as np
from jax import lax
from jax.experimental import pallas as pl
from jax.experimental.pallas import tpu as pltpu

F32 = jnp.float32
BF16 = jnp.bfloat16

D_MODEL = 2048
GRID_W = 64
EPS = 1e-6

A_WIDTH = D_MODEL // 2
GROUP_DIM = 128
A_GROUPS = A_WIDTH // GROUP_DIM
CHUNK = 128
B_WIDTH = D_MODEL // 2

D_INNER = 2 * D_MODEL
HEADDIM = 64
N_HEADS = D_INNER // HEADDIM
D_STATE = 128
N_BC_GROUPS = 8
HEADS_PER_GROUP = N_HEADS // N_BC_GROUPS
D_CONV = 5
SSD_CHUNK = 128
GN = N_BC_GROUPS * D_STATE
CONV_DIM = D_INNER + 2 * GN
GROUP_WIDTH = HEADS_PER_GROUP * HEADDIM

N_EXPERTS = 16
CAPACITY_FACTOR = 2

LANES = 128
BF16_SUBLANES = 16
VMEM_LIMIT = 56 * 1024 * 1024


def _params(*sem):
    return pltpu.CompilerParams(dimension_semantics=sem, vmem_limit_bytes=VMEM_LIMIT)


def _silu(x):
    return x / (1.0 + jnp.exp(-x))


def _split_bf16(x):
    hi = x.astype(BF16)
    lo = (x - hi.astype(F32)).astype(BF16)
    return hi, lo


def _dot(a, b):
    return jnp.dot(a, b, preferred_element_type=F32)


def _adaln_kernel(c_ref, w_ref, b_ref, o_ref):
    a = _silu(c_ref[...]).astype(BF16)
    o_ref[...] = _dot(a, w_ref[...].astype(BF16)) + b_ref[...]


def adaln(cond, mod_w, mod_b, layer):
    rows = cond.shape[0]
    tn = 1024
    n = 6 * D_MODEL
    return pl.pallas_call(
        _adaln_kernel,
        grid=(n // tn,),
        in_specs=[
            pl.BlockSpec((rows, D_MODEL), lambda j: (0, 0)),
            pl.BlockSpec((None, D_MODEL, tn), lambda j: (layer, 0, j)),
            pl.BlockSpec((None, 1, tn), lambda j: (layer, 0, j)),
        ],
        out_specs=pl.BlockSpec((rows, tn), lambda j: (0, j)),
        out_shape=jax.ShapeDtypeStruct((rows, n), F32),
        compiler_params=_params("arbitrary"),
        name="adaln",
    )(cond, mod_w, mod_b.reshape(mod_b.shape[0], 1, n))


def _rms_modulate(x, g, shift, scale):
    ms = jnp.mean(x * x, axis=-1, keepdims=True)
    y = x * lax.rsqrt(ms + EPS) * g
    return y * (1.0 + scale) + shift


def _modulate_kernel(x_ref, g_ref, sh_ref, sc_ref, o_ref):
    o_ref[0] = _rms_modulate(x_ref[0], g_ref[...], sh_ref[0], sc_ref[0]).astype(o_ref.dtype)


def _row_vec_spec(vec, n_batch):
    if vec.shape[0] == 1:
        return pl.BlockSpec((1, 1, D_MODEL), lambda b, i: (0, 0, 0))
    assert vec.shape[0] == n_batch
    return pl.BlockSpec((1, 1, D_MODEL), lambda b, i: (b, 0, 0))


def modulate(x, g, shift, scale):
    nb, t, _ = x.shape
    tm = min(t, 512)
    return pl.pallas_call(
        _modulate_kernel,
        grid=(nb, t // tm),
        in_specs=[
            pl.BlockSpec((1, tm, D_MODEL), lambda b, i: (b, i, 0)),
            pl.BlockSpec((1, D_MODEL), lambda b, i: (0, 0)),
            _row_vec_spec(shift, nb),
            _row_vec_spec(scale, nb),
        ],
        out_specs=pl.BlockSpec((1, tm, D_MODEL), lambda b, i: (b, i, 0)),
        out_shape=jax.ShapeDtypeStruct(x.shape, BF16),
        compiler_params=_params("arbitrary", "arbitrary"),
        name="modulate",
    )(x, g, shift, scale)


def _modulate_router_kernel(x_ref, g_ref, sh_ref, sc_ref, wr_ref, o_ref, p_ref):
    xm = _rms_modulate(x_ref[0], g_ref[...], sh_ref[0], sc_ref[0])
    xh, xl = _split_bf16(xm)
    wh, wl = _split_bf16(wr_ref[...])
    logits = _dot(xh, wh) + _dot(xl, wh) + _dot(xh, wl)
    lane = lax.broadcasted_iota(jnp.int32, logits.shape, 1)
    logits = jnp.where(lane < N_EXPERTS, logits, -jnp.inf)
    e = jnp.exp(logits - jnp.max(logits, axis=-1, keepdims=True))
    p_ref[0] = e / jnp.sum(e, axis=-1, keepdims=True)
    o_ref[0] = xh


def modulate_router(x, g, shift, scale, w_router_padded):
    nb, t, _ = x.shape
    tm = min(t, 512)
    return pl.pallas_call(
        _modulate_router_kernel,
        grid=(nb, t // tm),
        in_specs=[
            pl.BlockSpec((1, tm, D_MODEL), lambda b, i: (b, i, 0)),
            pl.BlockSpec((1, D_MODEL), lambda b, i: (0, 0)),
            _row_vec_spec(shift, nb),
            _row_vec_spec(scale, nb),
            pl.BlockSpec((D_MODEL, LANES), lambda b, i: (0, 0)),
        ],
        out_specs=[
            pl.BlockSpec((1, tm, D_MODEL), lambda b, i: (b, i, 0)),
            pl.BlockSpec((1, tm, LANES), lambda b, i: (b, i, 0)),
        ],
        out_shape=[
            jax.ShapeDtypeStruct(x.shape, BF16),
            jax.ShapeDtypeStruct((nb, t, LANES), F32),
        ],
        compiler_params=_params("arbitrary", "arbitrary"),
        name="modulate_router",
    )(x, g, shift, scale, w_router_padded)


def _matmul_kernel(*refs, n_a, n_extra, n_out, epilogue):
    a_refs = refs[:n_a]
    w_ref = refs[n_a]
    extra = refs[n_a + 1:n_a + 1 + n_extra]
    outs = refs[n_a + 1 + n_extra:n_a + 1 + n_extra + n_out]
    wbf_ref = refs[-1]

    @pl.when(pl.program_id(1) == 0)
    def _():
        wbf_ref[...] = w_ref[...].astype(BF16)

    acc = None
    off = 0
    for a_ref in a_refs:
        k = a_ref.shape[-1]
        part = _dot(a_ref[...], wbf_ref[off:off + k, :])
        acc = part if acc is None else acc + part
        off += k
    epilogue(acc, extra, outs)


def fused_matmul(a_list, w, layer, col0, ncols, tm, tn, epilogue, extras, extra_specs, out_dtypes, name):
    m = a_list[0].shape[0]
    k_total = w.shape[1]
    assert sum(a.shape[1] for a in a_list) == k_total
    assert m % tm == 0 and ncols % tn == 0 and col0 % tn == 0
    jb = col0 // tn
    in_specs = [pl.BlockSpec((tm, a.shape[1]), lambda j, i: (i, 0)) for a in a_list]
    in_specs.append(pl.BlockSpec((None, k_total, tn), lambda j, i: (layer, 0, jb + j)))
    in_specs.extend(extra_specs)
    out_specs = [pl.BlockSpec((tm, tn), lambda j, i: (i, j)) for _ in out_dtypes]
    out_shape = [jax.ShapeDtypeStruct((m, ncols), dt) for dt in out_dtypes]
    kern = functools.partial(_matmul_kernel, n_a=len(a_list), n_extra=len(extras),
                             n_out=len(out_dtypes), epilogue=epilogue)
    return pl.pallas_call(
        kern,
        grid=(ncols // tn, m // tm),
        in_specs=in_specs,
        out_specs=out_specs,
        out_shape=out_shape,
        scratch_shapes=[pltpu.VMEM((k_total, tn), BF16)],
        compiler_params=_params("arbitrary", "arbitrary"),
        name=name,
    )(*a_list, w, *extras)


def _epi_cast(acc, extra, outs):
    outs[0][...] = acc.astype(outs[0].dtype)


def _epi_gelu(acc, extra, outs):
    outs[0][...] = (0.5 * acc * (1.0 + lax.erf(acc * (1.0 / math.sqrt(2.0))))).astype(outs[0].dtype)


def _epi_dft(acc, extra, outs):
    ch_ref, cl_ref, sh_ref, sl_ref = extra
    a_ref, b_ref = outs
    zh, zl = _split_bf16(acc)
    for g in range(acc.shape[1] // GROUP_DIM):
        sl = slice(g * GROUP_DIM, (g + 1) * GROUP_DIM)
        a_ref[:, sl] = (_dot(zh[:, sl], ch_ref[...]) + _dot(zl[:, sl], ch_ref[...])
                        + _dot(zh[:, sl], cl_ref[...])).astype(a_ref.dtype)
        b_ref[:, sl] = (_dot(zh[:, sl], sh_ref[...]) + _dot(zl[:, sl], sh_ref[...])
                        + _dot(zh[:, sl], sl_ref[...])).astype(b_ref.dtype)


def _epi_residual(acc, extra, outs):
    h_ref, gate_ref = extra
    outs[0][...] = h_ref[...] + gate_ref[0] * acc


def _residual_specs(h_flat, gate, t, tm, tn):
    per_batch = t // tm
    if gate.shape[0] == 1:
        gate_spec = pl.BlockSpec((1, 1, tn), lambda j, i: (0, 0, j))
    else:
        gate_spec = pl.BlockSpec((1, 1, tn), lambda j, i: (i // per_batch, 0, j))
    return [pl.BlockSpec((tm, tn), lambda j, i: (i, j)), gate_spec]


def _gmlp_kernel(u_ref, v_ref, vg_ref, ws_ref, bs_ref, o_ref):
    v = v_ref[0].astype(F32)
    vn = (v * lax.rsqrt(jnp.mean(v * v, axis=-1, keepdims=True) + EPS) * vg_ref[...]).astype(BF16)
    for g in range(A_GROUPS):
        sl = slice(g * GROUP_DIM, (g + 1) * GROUP_DIM)
        s = _dot(ws_ref[g].astype(BF16), vn[:, sl]) + bs_ref[:, g:g + 1]
        o_ref[0, :, sl] = (u_ref[0, :, sl].astype(F32) * s).astype(o_ref.dtype)


def chunk_gmlp(a, v_g, w_s, b_s_t, layer):
    nb, t, _ = a.shape
    return pl.pallas_call(
        _gmlp_kernel,
        grid=(nb, t // CHUNK),
        in_specs=[
            pl.BlockSpec((1, CHUNK, A_WIDTH), lambda b, c: (b, c, 0)),
            pl.BlockSpec((1, CHUNK, A_WIDTH), lambda b, c: (b, c, 1)),
            pl.BlockSpec((None, 1, A_WIDTH), lambda b, c: (layer, 0, 0)),
            pl.BlockSpec((None, A_GROUPS, CHUNK, CHUNK), lambda b, c: (layer, 0, 0, 0)),
            pl.BlockSpec((None, CHUNK, A_GROUPS), lambda b, c: (layer, 0, 0)),
        ],
        out_specs=pl.BlockSpec((1, CHUNK, A_WIDTH), lambda b, c: (b, c, 0)),
        out_shape=jax.ShapeDtypeStruct((nb, t, A_WIDTH), BF16),
        compiler_params=_params("arbitrary", "arbitrary"),
        name="chunk_gmlp",
    )(a, a, v_g.reshape(v_g.shape[0], 1, A_WIDTH), w_s, b_s_t)


def _dft_tables(n):
    k = np.arange(n, dtype=np.int64)
    ang = 2.0 * np.pi * ((k[:, None] * k[None, :]) % n).astype(np.float64) / n
    return np.cos(ang), np.sin(ang)


def _hi_lo(table):
    hi = jnp.asarray(table, dtype=F32).astype(BF16)
    lo = (jnp.asarray(table, dtype=F32) - hi.astype(F32)).astype(BF16)
    return hi, lo


def _token_dft_kernel(ct_ref, st_ref, a_ref, b_ref, o_ref, *, scale):
    y = _dot(ct_ref[...], a_ref[0]) - _dot(st_ref[...], b_ref[0])
    o_ref[0] = (y * scale).astype(o_ref.dtype)


def token_dft(a, b):
    nb, t, w = a.shape
    cos_t, sin_t = _dft_tables(t)
    ct = jnp.asarray(cos_t, dtype=F32).astype(BF16)
    st = jnp.asarray(sin_t, dtype=F32).astype(BF16)
    tm = min(t, 512)
    kern = functools.partial(_token_dft_kernel, scale=1.0 / math.sqrt(t * GROUP_DIM))
    return pl.pallas_call(
        kern,
        grid=(nb, t // tm),
        in_specs=[
            pl.BlockSpec((tm, t), lambda b_, i: (i, 0)),
            pl.BlockSpec((tm, t), lambda b_, i: (i, 0)),
            pl.BlockSpec((1, t, w), lambda b_, i: (b_, 0, 0)),
            pl.BlockSpec((1, t, w), lambda b_, i: (b_, 0, 0)),
        ],
        out_specs=pl.BlockSpec((1, tm, w), lambda b_, i: (b_, i, 0)),
        out_shape=jax.ShapeDtypeStruct((nb, t, w), BF16),
        compiler_params=_params("arbitrary", "arbitrary"),
        name="token_dft",
    )(ct, st, a, b)


def mixer_ab(xn, h, gate, ab_w_in, ab_w_out, gm_v_g, gm_w_s, gm_b_s_t, layer):
    nb, t, _ = xn.shape
    m = nb * t
    xf = xn.reshape(m, D_MODEL)
    tm = min(m, 512)
    (a,) = fused_matmul([xf], ab_w_in, layer, 0, 2 * A_WIDTH, tm, 1024, _epi_gelu, [], [], [BF16], "ab_in_gelu")
    c128, s128 = _dft_tables(GROUP_DIM)
    tables = [*_hi_lo(c128), *_hi_lo(s128)]
    table_specs = [pl.BlockSpec((GROUP_DIM, GROUP_DIM), lambda j, i: (0, 0)) for _ in tables]
    za, zb = fused_matmul([xf], ab_w_in, layer, 2 * A_WIDTH, B_WIDTH, tm, 1024, _epi_dft, tables, table_specs,
                          [BF16, BF16], "ab_in_dft")
    ya = chunk_gmlp(a.reshape(nb, t, 2 * A_WIDTH), gm_v_g, gm_w_s, gm_b_s_t, layer)
    yb = token_dft(za.reshape(nb, t, B_WIDTH), zb.reshape(nb, t, B_WIDTH))
    hf = h.reshape(m, D_MODEL)
    tn = 1024
    (out,) = fused_matmul([ya.reshape(m, A_WIDTH), yb.reshape(m, B_WIDTH)], ab_w_out, layer, 0, D_MODEL, tm, tn,
                          _epi_residual, [hf, gate], _residual_specs(hf, gate, t, tm, tn), [F32], "ab_out")
    return out.reshape(nb, t, D_MODEL)


def _conv_kernel(prev_ref, main_ref, next_ref, w_ref, b_ref, o_ref, *, rows):
    r = pl.program_id(1)
    last = pl.num_programs(1) - 1
    prev = jnp.where(r > 0, prev_ref[0].astype(F32), 0.0)
    nxt = jnp.where(r < last, next_ref[0].astype(F32), 0.0)
    full = jnp.concatenate([prev, main_ref[0].astype(F32), nxt], axis=0)
    n = full.shape[0]
    w = w_ref[...]
    acc = None
    for k in range(D_CONV):
        shift = (D_CONV // 2 - k) % n
        term = (full if shift == 0 else pltpu.roll(full, shift, 0)) * w[k:k + 1, :]
        acc = term if acc is None else acc + term
    y = acc[BF16_SUBLANES:BF16_SUBLANES + rows, :] + b_ref[...]
    o_ref[0] = _silu(y).astype(o_ref.dtype)


def dwconv_silu(z, conv_w, conv_b, layer):
    nb, t, ch = z.shape
    rows = min(t, 512)
    tc = 1024
    halo = BF16_SUBLANES
    per = rows // halo
    n_halo = t // halo
    kern = functools.partial(_conv_kernel, rows=rows)
    return pl.pallas_call(
        kern,
        grid=(nb, t // rows, ch // tc),
        in_specs=[
            pl.BlockSpec((1, halo, tc), lambda b, r, c: (b, jnp.maximum(r * per - 1, 0), c)),
            pl.BlockSpec((1, rows, tc), lambda b, r, c: (b, r, c)),
            pl.BlockSpec((1, halo, tc), lambda b, r, c: (b, jnp.minimum((r + 1) * per, n_halo - 1), c)),
            pl.BlockSpec((None, D_CONV, tc), lambda b, r, c: (layer, 0, c)),
            pl.BlockSpec((None, 1, tc), lambda b, r, c: (layer, 0, c)),
        ],
        out_specs=pl.BlockSpec((1, rows, tc), lambda b, r, c: (b, r, c)),
        out_shape=jax.ShapeDtypeStruct(z.shape, BF16),
        compiler_params=_params("arbitrary", "arbitrary", "arbitrary"),
        name="dwconv_silu",
    )(z, z, z, conv_w, conv_b.reshape(conv_b.shape[0], 1, ch))


def _softplus(x):
    return jnp.maximum(x, 0.0) + jnp.log(1.0 + jnp.exp(-jnp.abs(x)))


def _prefix_sum(x, axis):
    n = x.shape[axis]
    idx = lax.broadcasted_iota(jnp.int32, x.shape, axis)
    s = 1
    while s < n:
        x = x + jnp.where(idx >= s, pltpu.roll(x, s, axis), 0.0)
        s *= 2
    return x


def _pair_expand(col_a, col_b, lane_lo):
    shape = lane_lo.shape
    return jnp.where(lane_lo, jnp.broadcast_to(col_a, shape), jnp.broadcast_to(col_b, shape))


def _ssd_chunk(c, direction, x_ref, b_ref, c_ref, dtc_ref, dtr_ref, bias_c_ref, bias_r_ref,
               a_c_ref, a_r_ref, dskip_ref, y_ref, state_ref):
    L = SSD_CHUNK
    hp = HEADS_PER_GROUP
    rows = pl.ds(pl.multiple_of(c * L, L), L)
    lanes = pl.ds(pl.multiple_of(c * L, L), L)
    hs = slice(direction * hp, (direction + 1) * hp)

    dt_c = _softplus(dtc_ref[0, 0, rows, :] + bias_c_ref[0])
    dt_r = _softplus(dtr_ref[0, 0, :, lanes] + bias_r_ref[0])
    la_c = dt_c * a_c_ref[0]
    la_r = dt_r * a_r_ref[0]
    cum_c = _prefix_sum(la_c, 0)
    cum_r = _prefix_sum(la_r, 1)
    tot_c = cum_c[L - 1:L, :]
    tot_r = cum_r[:, L - 1:L]
    if direction == 1:
        cum_c = tot_c - cum_c + la_c
        cum_r = tot_r - cum_r + la_r
    w_c = jnp.exp(tot_c - cum_c) * dt_c
    edec_c = jnp.exp(cum_c)
    dec_r = jnp.exp(tot_r)

    bm = b_ref[0, rows, :]
    cm = c_ref[0, rows, :]
    cb = lax.dot_general(cm, bm, (((1,), (1,)), ((), ())), preferred_element_type=F32)
    bm_t = bm.astype(F32).T.astype(BF16)

    ii = lax.broadcasted_iota(jnp.int32, (L, L), 0)
    jj = lax.broadcasted_iota(jnp.int32, (L, L), 1)
    causal = (jj <= ii) if direction == 0 else (jj >= ii)
    lane_lo = lax.broadcasted_iota(jnp.int32, (L, 2 * HEADDIM), 1) < HEADDIM
    row_lo = lax.broadcasted_iota(jnp.int32, (1, 2 * HEADDIM), 1) < HEADDIM

    for p in range(hp // 2):
        k1 = direction * hp + 2 * p
        k2 = k1 + 1
        cols = slice(p * 2 * HEADDIM, (p + 1) * 2 * HEADDIM)
        xp = x_ref[0, rows, cols]
        xpf = xp.astype(F32)
        ms = []
        for k in (k1, k2):
            seg = cum_c[:, k:k + 1] - cum_r[k:k + 1, :]
            decay = jnp.exp(jnp.where(causal, seg, -jnp.inf))
            ms.append((cb * decay * dt_r[k:k + 1, :]).astype(BF16))
        lhs2 = jnp.concatenate(ms, axis=1)
        zero = jnp.zeros_like(xp)
        rhs2 = jnp.concatenate([jnp.where(lane_lo, xp, zero), jnp.where(lane_lo, zero, xp)], axis=0)
        y = _dot(lhs2, rhs2)
        s_prev = state_ref[direction, :, cols]
        y = y + _dot(cm, s_prev.astype(BF16)) * _pair_expand(edec_c[:, k1:k1 + 1], edec_c[:, k2:k2 + 1], lane_lo)
        wx = (xpf * _pair_expand(w_c[:, k1:k1 + 1], w_c[:, k2:k2 + 1], lane_lo)).astype(BF16)
        dec = jnp.where(row_lo, jnp.broadcast_to(dec_r[k1:k1 + 1, :], row_lo.shape),
                        jnp.broadcast_to(dec_r[k2:k2 + 1, :], row_lo.shape))
        state_ref[direction, :, cols] = s_prev * dec + _dot(bm_t, wx)
        if direction == 0:
            y_ref[0, rows, cols] = y + xpf * dskip_ref[0, :, cols]
        else:
            y_ref[0, rows, cols] = y_ref[0, rows, cols] + y


def _ssd_kernel(x_ref, b_ref, c_ref, dtc_ref, dtr_ref, bias_c_ref, bias_r_ref, a_c_ref, a_r_ref,
                dskip_ref, s0_ref, y_ref, sfin_ref, state_ref, *, n_chunks):
    state_ref[...] = s0_ref[:, 0, 0]
    args = (x_ref, b_ref, c_ref, dtc_ref, dtr_ref, bias_c_ref, bias_r_ref, a_c_ref, a_r_ref,
            dskip_ref, y_ref, state_ref)

    def fwd(c, carry):
        _ssd_chunk(c, 0, *args)
        return carry

    def bwd(c, carry):
        _ssd_chunk(n_chunks - 1 - c, 1, *args)
        return carry

    lax.fori_loop(0, n_chunks, fwd, 0)
    lax.fori_loop(0, n_chunks, bwd, 0)
    sfin_ref[:, 0, 0] = state_ref[...]


def ssd_scan(xbc, dt_raw, dt_bias, a, d_skip, s0):
    nb, t, _ = xbc.shape
    g, hp = N_BC_GROUPS, HEADS_PER_GROUP
    dt_g = dt_raw.reshape(nb, t, 2, g, hp).transpose(0, 3, 1, 2, 4).reshape(nb, g, t, 2 * hp)
    dt_gt = dt_g.transpose(0, 1, 3, 2)
    bias_g = dt_bias.reshape(2, g, hp).transpose(1, 0, 2).reshape(g, 1, 2 * hp)
    a_g = a.reshape(2, g, hp).transpose(1, 0, 2).reshape(g, 1, 2 * hp)
    dskip = jnp.repeat(d_skip, HEADDIM).reshape(g, 1, GROUP_WIDTH)
    x_blk = D_INNER // GROUP_WIDTH
    b_off = D_INNER // D_STATE
    c_off = (D_INNER + GN) // D_STATE
    kern = functools.partial(_ssd_kernel, n_chunks=t // SSD_CHUNK)
    y, s_fin = pl.pallas_call(
        kern,
        grid=(nb, g),
        in_specs=[
            pl.BlockSpec((1, t, GROUP_WIDTH), lambda b, gi: (b, 0, gi)),
            pl.BlockSpec((1, t, D_STATE), lambda b, gi: (b, 0, b_off + gi)),
            pl.BlockSpec((1, t, D_STATE), lambda b, gi: (b, 0, c_off + gi)),
            pl.BlockSpec((1, 1, t, 2 * hp), lambda b, gi: (b, gi, 0, 0)),
            pl.BlockSpec((1, 1, 2 * hp, t), lambda b, gi: (b, gi, 0, 0)),
            pl.BlockSpec((1, 1, 2 * hp), lambda b, gi: (gi, 0, 0)),
            pl.BlockSpec((1, 2 * hp, 1), lambda b, gi: (gi, 0, 0)),
            pl.BlockSpec((1, 1, 2 * hp), lambda b, gi: (gi, 0, 0)),
            pl.BlockSpec((1, 2 * hp, 1), lambda b, gi: (gi, 0, 0)),
            pl.BlockSpec((1, 1, GROUP_WIDTH), lambda b, gi: (gi, 0, 0)),
            pl.BlockSpec((2, 1, 1, D_STATE, GROUP_WIDTH), lambda b, gi: (0, b, gi, 0, 0)),
        ],
        out_specs=[
            pl.BlockSpec((1, t, GROUP_WIDTH), lambda b, gi: (b, 0, gi)),
            pl.BlockSpec((2, 1, 1, D_STATE, GROUP_WIDTH), lambda b, gi: (0, b, gi, 0, 0)),
        ],
        out_shape=[
            jax.ShapeDtypeStruct((nb, t, D_INNER), F32),
            jax.ShapeDtypeStruct(s0.shape, F32),
        ],
        scratch_shapes=[pltpu.VMEM((2, D_STATE, GROUP_WIDTH), F32)],
        compiler_params=_params("arbitrary", "arbitrary"),
        name="ssd_scan",
    )(xbc, xbc, xbc, dt_g, dt_gt, bias_g, bias_g.transpose(0, 2, 1), a_g, a_g.transpose(0, 2, 1), dskip, s0)
    del x_blk
    return y, s_fin


def _gate_norm_kernel(y_ref, z_ref, g_ref, o_ref):
    y = y_ref[...] * _silu(z_ref[...].astype(F32))
    o_ref[...] = (y * lax.rsqrt(jnp.mean(y * y, axis=-1, keepdims=True) + EPS) * g_ref[...]).astype(o_ref.dtype)


def gate_norm(y, z, norm_g, layer):
    m = y.shape[0]
    tm = min(m, 256)
    return pl.pallas_call(
        _gate_norm_kernel,
        grid=(m // tm,),
        in_specs=[
            pl.BlockSpec((tm, D_INNER), lambda i: (i, 0)),
            pl.BlockSpec((tm, D_INNER), lambda i: (i, 0)),
            pl.BlockSpec((None, 1, D_INNER), lambda i: (layer, 0, 0)),
        ],
        out_specs=pl.BlockSpec((tm, D_INNER), lambda i: (i, 0)),
        out_shape=jax.ShapeDtypeStruct((m, D_INNER), BF16),
        compiler_params=_params("arbitrary"),
        name="gate_norm",
    )(y, z, norm_g.reshape(norm_g.shape[0], 1, D_INNER))


def ssd_project(xn, w_in, conv_w, conv_b, layer):
    nb, t, _ = xn.shape
    m = nb * t
    xf = xn.reshape(m, D_MODEL)
    tm = min(m, 512)
    (z,) = fused_matmul([xf], w_in, layer, 0, D_INNER, tm, 1024, _epi_cast, [], [], [BF16], "ssd_in_z")
    (xbc,) = fused_matmul([xf], w_in, layer, D_INNER, CONV_DIM, tm, 1024, _epi_cast, [], [], [BF16], "ssd_in_xbc")
    (dt_raw,) = fused_matmul([xf], w_in, layer, D_INNER + CONV_DIM, 2 * N_HEADS, tm, 2 * N_HEADS, _epi_cast,
                             [], [], [F32], "ssd_in_dt")
    xbc = dwconv_silu(xbc.reshape(nb, t, CONV_DIM), conv_w, conv_b, layer)
    return z, xbc, dt_raw.reshape(nb, t, 2 * N_HEADS)


def ssd_mix(xn, xc, h, gate, w_in, conv_w, conv_b, dt_bias, a_log, d_skip, norm_g, w_out, layer):
    nb, t, _ = xn.shape
    m = nb * t
    a = -jnp.exp(a_log[layer].astype(F32))
    zero = jnp.zeros((2, nb, N_BC_GROUPS, D_STATE, GROUP_WIDTH), F32)
    _, xbc_c, dt_c = ssd_project(xc, w_in, conv_w, conv_b, layer)
    _, s_ctx = ssd_scan(xbc_c, dt_c, dt_bias[layer], a, d_skip[layer], zero)
    z, xbc, dt_raw = ssd_project(xn, w_in, conv_w, conv_b, layer)
    y, _ = ssd_scan(xbc, dt_raw, dt_bias[layer], a, d_skip[layer], s_ctx)
    yn = gate_norm(y.reshape(m, D_INNER), z, norm_g, layer)
    hf = h.reshape(m, D_MODEL)
    tm, tn = 512, 512
    (out,) = fused_matmul([yn], w_out, layer, 0, D_MODEL, tm, tn, _epi_residual, [hf, gate],
                          _residual_specs(hf, gate, t, tm, tn), [F32], "ssd_out")
    return out.reshape(nb, t, D_MODEL)


def _route_kernel(p_ref, tri_ref, pos_ref, post_ref, gate_ref, *, cap):
    probs = p_ref[0]
    t = probs.shape[0]
    pe = probs.T[:N_EXPERTS, :]
    bits = pltpu.bitcast(pe, jnp.int32)

    def step(i, v):
        cand = v | jnp.left_shift(jnp.int32(1), 30 - i)
        cnt = jnp.sum((bits >= cand).astype(F32), axis=1, keepdims=True)
        return jnp.where(cnt >= cap, cand, v)

    thr = lax.fori_loop(0, 31, step, jnp.zeros((N_EXPERTS, 1), jnp.int32))
    gt = bits > thr
    eq = bits == thr
    need = cap - jnp.sum(gt.astype(F32), axis=1, keepdims=True)
    tri = tri_ref[...]
    eq_rank = _dot(eq.astype(BF16), tri)
    sel = gt | (eq & (eq_rank <= need))
    pos = _dot(sel.astype(BF16), tri) - 1.0
    pos = jnp.where(sel, pos, -1.0)
    pos_ref[0] = pos
    pos_full = jnp.concatenate([pos, jnp.full((LANES - N_EXPERTS, t), -1.0, F32)], axis=0)
    pos_t = pos_full.T
    post_ref[0] = pos_t
    gate_ref[0] = jnp.where(pos_t >= 0.0, probs, 0.0)


def route(probs, cap):
    nb, t, _ = probs.shape
    idx = np.arange(t)
    tri = jnp.asarray((idx[:, None] <= idx[None, :]).astype(np.float32), dtype=BF16)
    kern = functools.partial(_route_kernel, cap=cap)
    return pl.pallas_call(
        kern,
        grid=(nb,),
        in_specs=[
            pl.BlockSpec((1, t, LANES), lambda b: (b, 0, 0)),
            pl.BlockSpec((t, t), lambda b: (0, 0)),
        ],
        out_specs=[
            pl.BlockSpec((1, N_EXPERTS, t), lambda b: (b, 0, 0)),
            pl.BlockSpec((1, t, LANES), lambda b: (b, 0, 0)),
            pl.BlockSpec((1, t, LANES), lambda b: (b, 0, 0)),
        ],
        out_shape=[
            jax.ShapeDtypeStruct((nb, N_EXPERTS, t), F32),
            jax.ShapeDtypeStruct((nb, t, LANES), F32),
            jax.ShapeDtypeStruct((nb, t, LANES), F32),
        ],
        compiler_params=_params("arbitrary"),
        name="route",
    )(probs, tri)


def _gather_kernel(pos_ref, x_ref, o_ref, *, cap):
    pos = pos_ref[0, 0]
    t = pos.shape[1]
    slot = lax.broadcasted_iota(jnp.int32, (cap, t), 0).astype(F32)
    onehot = (slot == pos).astype(BF16)
    o_ref[0] = _dot(onehot, x_ref[0]).astype(o_ref.dtype)


def gather_tokens(pos, xm, cap):
    nb, t, _ = xm.shape
    kern = functools.partial(_gather_kernel, cap=cap)
    return pl.pallas_call(
        kern,
        grid=(nb, N_EXPERTS),
        in_specs=[
            pl.BlockSpec((1, 1, 1, t), lambda b, e: (b, e, 0, 0)),
            pl.BlockSpec((1, t, D_MODEL), lambda b, e: (b, 0, 0)),
        ],
        out_specs=pl.BlockSpec((1, cap, D_MODEL), lambda b, e: (e, b, 0)),
        out_shape=jax.ShapeDtypeStruct((N_EXPERTS, nb * cap, D_MODEL), BF16),
        compiler_params=_params("arbitrary", "arbitrary"),
        name="gather_tokens",
    )(pos.reshape(nb, N_EXPERTS, 1, t), xm)


def _ffn_kernel(x_ref, wg_ref, wu_ref, wd_ref, o_ref, acc_ref):
    f = pl.program_id(2)
    x = x_ref[0]
    gate = _dot(x, wg_ref[...].astype(BF16))
    up = _dot(x, wu_ref[...].astype(BF16))
    hid = (_silu(gate) * up).astype(BF16)
    part = _dot(hid, wd_ref[...].astype(BF16))

    @pl.when(f == 0)
    def _():
        acc_ref[...] = part

    @pl.when(f > 0)
    def _():
        acc_ref[...] += part

    @pl.when(f == pl.num_programs(2) - 1)
    def _():
        o_ref[0] = acc_ref[...].astype(o_ref.dtype)


def expert_ffn(xe, w_gate, w_up, w_down, layer):
    _, m, _ = xe.shape
    tm = min(m, 1024)
    tf = 256
    return pl.pallas_call(
        _ffn_kernel,
        grid=(N_EXPERTS, m // tm, D_MODEL // tf),
        in_specs=[
            pl.BlockSpec((1, tm, D_MODEL), lambda e, i, f: (e, i, 0)),
            pl.BlockSpec((None, None, D_MODEL, tf), lambda e, i, f: (layer, e, 0, f)),
            pl.BlockSpec((None, None, D_MODEL, tf), lambda e, i, f: (layer, e, 0, f)),
            pl.BlockSpec((None, None, tf, D_MODEL), lambda e, i, f: (layer, e, f, 0)),
        ],
        out_specs=pl.BlockSpec((1, tm, D_MODEL), lambda e, i, f: (e, i, 0)),
        out_shape=jax.ShapeDtypeStruct(xe.shape, BF16),
        scratch_shapes=[pltpu.VMEM((tm, D_MODEL), F32)],
        compiler_params=_params("arbitrary", "arbitrary", "arbitrary"),
        name="expert_ffn",
    )(xe, w_gate, w_up, w_down)


def _combine_kernel(post_ref, gate_ref, y_ref, h_ref, g2_ref, o_ref, acc_ref, *, cap):
    e = pl.program_id(2)
    tq = post_ref.shape[1]
    lane = lax.broadcasted_iota(jnp.int32, (tq, LANES), 1)
    pick = lane == e
    pos_col = jnp.sum(jnp.where(pick, post_ref[0], 0.0), axis=1, keepdims=True)
    gate_col = jnp.sum(jnp.where(pick, gate_ref[0], 0.0), axis=1, keepdims=True)
    slot = lax.broadcasted_iota(jnp.int32, (tq, cap), 1).astype(F32)
    onehot = (slot == pos_col).astype(BF16)
    contrib = gate_col * _dot(onehot, y_ref[0])

    @pl.when(e == 0)
    def _():
        acc_ref[...] = contrib

    @pl.when(e > 0)
    def _():
        acc_ref[...] += contrib

    @pl.when(e == N_EXPERTS - 1)
    def _():
        o_ref[0] = h_ref[0] + g2_ref[0] * acc_ref[...]


def combine(pos_t, gate_t, ye, h, g2, cap):
    nb, t, _ = h.shape
    tq = min(t, 512)
    if g2.shape[0] == 1:
        g2_spec = pl.BlockSpec((1, 1, D_MODEL), lambda b, i, e: (0, 0, 0))
    else:
        g2_spec = pl.BlockSpec((1, 1, D_MODEL), lambda b, i, e: (b, 0, 0))
    kern = functools.partial(_combine_kernel, cap=cap)
    return pl.pallas_call(
        kern,
        grid=(nb, t // tq, N_EXPERTS),
        in_specs=[
            pl.BlockSpec((1, tq, LANES), lambda b, i, e: (b, i, 0)),
            pl.BlockSpec((1, tq, LANES), lambda b, i, e: (b, i, 0)),
            pl.BlockSpec((1, cap, D_MODEL), lambda b, i, e: (e, b, 0)),
            pl.BlockSpec((1, tq, D_MODEL), lambda b, i, e: (b, i, 0)),
            g2_spec,
        ],
        out_specs=pl.BlockSpec((1, tq, D_MODEL), lambda b, i, e: (b, i, 0)),
        out_shape=jax.ShapeDtypeStruct(h.shape, F32),
        scratch_shapes=[pltpu.VMEM((tq, D_MODEL), F32)],
        compiler_params=_params("arbitrary", "arbitrary", "arbitrary"),
        name="combine",
    )(pos_t, gate_t, ye, h, g2)


def expert_choice_ffn(h, g, shift, scale, g2, w_router, w_gate, w_up, w_down, layer):
    nb, t, _ = h.shape
    cap = CAPACITY_FACTOR * t // N_EXPERTS
    wr = jnp.pad(w_router[layer], ((0, 0), (0, LANES - N_EXPERTS)))
    xm, probs = modulate_router(h, g, shift, scale, wr)
    pos, pos_t, gate_t = route(probs, cap)
    xe = gather_tokens(pos, xm, cap)
    ye = expert_ffn(xe, w_gate, w_up, w_down, layer)
    return combine(pos_t, gate_t, ye, h, g2, cap)


def _rmsnorm_kernel(x_ref, g_ref, o_ref):
    x = x_ref[...]
    o_ref[...] = x * lax.rsqrt(jnp.mean(x * x, axis=-1, keepdims=True) + EPS) * g_ref[...]


def rmsnorm(x, g):
    m = x.shape[0]
    tm = min(m, 512)
    return pl.pallas_call(
        _rmsnorm_kernel,
        grid=(m // tm,),
        in_specs=[pl.BlockSpec((tm, D_MODEL), lambda i: (i, 0)), pl.BlockSpec((1, D_MODEL), lambda i: (0, 0))],
        out_specs=pl.BlockSpec((tm, D_MODEL), lambda i: (i, 0)),
        out_shape=jax.ShapeDtypeStruct(x.shape, F32),
        compiler_params=_params("arbitrary"),
        name="final_rmsnorm",
    )(x, g)


def _sincos_2d(rows, cols, dim):
    quarter = dim // 4
    omega = 1.0 / (10000.0 ** (jnp.arange(quarter, dtype=F32) / quarter))
    r = jnp.repeat(jnp.arange(rows, dtype=F32), cols)[:, None] * omega
    cl = jnp.tile(jnp.arange(cols, dtype=F32), rows)[:, None] * omega
    return jnp.concatenate([jnp.sin(r), jnp.cos(r), jnp.sin(cl), jnp.cos(cl)], axis=-1)


def kernel(x, c, ctx, c_ctx, mod_w, mod_b, norm_g, final_g, ab_w_in, ab_w_out, gm_v_g, gm_w_s, gm_b_s,
           ssd_w_in, ssd_conv_w, ssd_conv_b, ssd_dt_bias, ssd_a_log, ssd_d, ssd_norm_g, ssd_w_out,
           moe_w_router, moe_w_gate, moe_w_up, moe_w_down):
    nb, n, _ = x.shape
    depth = mod_w.shape[0]
    h = x + _sincos_2d(n // GRID_W, GRID_W, D_MODEL)[None]
    hc = ctx
    cond = jnp.concatenate([c, c_ctx[None], jnp.zeros((16 - nb - 1, D_MODEL), F32)], axis=0)
    gm_b_s_t = jnp.swapaxes(gm_b_s, 1, 2)
    for i in range(depth):
        last = i == depth - 1
        even = i % 2 == 0
        j = i // 2
        ctx_reaches_latent = (not last) or (not even)
        mod = adaln(cond, mod_w, mod_b, i).reshape(16, 6, 1, D_MODEL)
        sh1, sc1, g1, sh2, sc2, g2 = (mod[:nb, k] for k in range(6))
        csh1, csc1, cg1, csh2, csc2, cg2 = (mod[nb:nb + 1, k] for k in range(6))
        g_mix = norm_g[i, 0][None]
        g_ffn = norm_g[i, 1][None]
        xn = modulate(h, g_mix, sh1, sc1)
        if ctx_reaches_latent:
            xc = modulate(hc, g_mix, csh1, csc1)
        if even:
            h = mixer_ab(xn, h, g1, ab_w_in, ab_w_out, gm_v_g, gm_w_s, gm_b_s_t, j)
            if not last:
                hc = mixer_ab(xc, hc, cg1, ab_w_in, ab_w_out, gm_v_g, gm_w_s, gm_b_s_t, j)
        else:
            assert last, "context outputs of a state-space layer are only needed before the last layer"
            h = ssd_mix(xn, xc, h, g1, ssd_w_in, ssd_conv_w, ssd_conv_b, ssd_dt_bias, ssd_a_log, ssd_d,
                        ssd_norm_g, ssd_w_out, j)
        h = expert_choice_ffn(h, g_ffn, sh2, sc2, g2, moe_w_router, moe_w_gate, moe_w_up, moe_w_down, i)
        if not last:
            hc = expert_choice_ffn(hc, g_ffn, csh2, csc2, cg2, moe_w_router, moe_w_gate, moe_w_up, moe_w_down, i)
    return rmsnorm(h.reshape(nb * n, D_MODEL), final_g[None]).reshape(nb, n, D_MODEL)
```

```python
import functools
import math

import jax
import jax.numpy as jnp
import numpy as np
from jax import lax
from jax.experimental import pallas as pl
from jax.experimental.pallas import tpu as pltpu

F32 = jnp.float32
BF16 = jnp.bfloat16

D_MODEL = 2048
GRID_W = 64
EPS = 1e-6

A_WIDTH = D_MODEL // 2
GROUP_DIM = 128
A_GROUPS = A_WIDTH // GROUP_DIM
CHUNK = 128
B_WIDTH = D_MODEL // 2

D_INNER = 2 * D_MODEL
HEADDIM = 64
N_HEADS = D_INNER // HEADDIM
D_STATE = 128
N_BC_GROUPS = 8
HEADS_PER_GROUP = N_HEADS // N_BC_GROUPS
D_CONV = 5
SSD_CHUNK = 128
GN = N_BC_GROUPS * D_STATE
CONV_DIM = D_INNER + 2 * GN
GROUP_WIDTH = HEADS_PER_GROUP * HEADDIM

N_EXPERTS = 16
CAPACITY_FACTOR = 2

LANES = 128
BF16_SUBLANES = 16
VMEM_LIMIT = 56 * 1024 * 1024
LOG2E = 1.4426950408889634


def _params(*sem):
    return pltpu.CompilerParams(dimension_semantics=sem, vmem_limit_bytes=VMEM_LIMIT)


def _silu(x):
    return x / (1.0 + jnp.exp(-x))


def _split_bf16(x):
    hi = x.astype(BF16)
    lo = (x - hi.astype(F32)).astype(BF16)
    return hi, lo


def _dot(a, b):
    return jnp.dot(a, b, preferred_element_type=F32)


def _adaln_kernel(c_ref, w_ref, b_ref, o_ref):
    a = _silu(c_ref[...]).astype(BF16)
    o_ref[...] = _dot(a, w_ref[...].astype(BF16)) + b_ref[...]


def adaln(cond, mod_w, mod_b, layer):
    rows = cond.shape[0]
    tn = 1024
    n = 6 * D_MODEL
    return pl.pallas_call(
        _adaln_kernel,
        grid=(n // tn,),
        in_specs=[
            pl.BlockSpec((rows, D_MODEL), lambda j: (0, 0)),
            pl.BlockSpec((None, D_MODEL, tn), lambda j: (layer, 0, j)),
            pl.BlockSpec((None, 1, tn), lambda j: (layer, 0, j)),
        ],
        out_specs=pl.BlockSpec((rows, tn), lambda j: (0, j)),
        out_shape=jax.ShapeDtypeStruct((rows, n), F32),
        compiler_params=_params("arbitrary"),
        name="adaln",
    )(cond, mod_w, mod_b.reshape(mod_b.shape[0], 1, n))


def _rms_modulate(x, g, shift, scale):
    ms = jnp.mean(x * x, axis=-1, keepdims=True)
    y = x * lax.rsqrt(ms + EPS) * g
    return y * (1.0 + scale) + shift


def _modulate_kernel(x_ref, g_ref, sh_ref, sc_ref, o_ref):
    o_ref[0] = _rms_modulate(x_ref[0], g_ref[...], sh_ref[0], sc_ref[0]).astype(o_ref.dtype)


def _row_vec_spec(vec, n_batch):
    if vec.shape[0] == 1:
        return pl.BlockSpec((1, 1, D_MODEL), lambda b, i: (0, 0, 0))
    assert vec.shape[0] == n_batch
    return pl.BlockSpec((1, 1, D_MODEL), lambda b, i: (b, 0, 0))


def modulate(x, g, shift, scale):
    nb, t, _ = x.shape
    tm = min(t, 512)
    return pl.pallas_call(
        _modulate_kernel,
        grid=(nb, t // tm),
        in_specs=[
            pl.BlockSpec((1, tm, D_MODEL), lambda b, i: (b, i, 0)),
            pl.BlockSpec((1, D_MODEL), lambda b, i: (0, 0)),
            _row_vec_spec(shift, nb),
            _row_vec_spec(scale, nb),
        ],
        out_specs=pl.BlockSpec((1, tm, D_MODEL), lambda b, i: (b, i, 0)),
        out_shape=jax.ShapeDtypeStruct(x.shape, BF16),
        compiler_params=_params("arbitrary", "arbitrary"),
        name="modulate",
    )(x, g, shift, scale)


def _modulate_router_kernel(x_ref, g_ref, sh_ref, sc_ref, wr_ref, o_ref, p_ref):
    xm = _rms_modulate(x_ref[0], g_ref[...], sh_ref[0], sc_ref[0])
    xh, xl = _split_bf16(xm)
    wh, wl = _split_bf16(wr_ref[...])
    logits = _dot(xh, wh) + _dot(xl, wh) + _dot(xh, wl)
    lane = lax.broadcasted_iota(jnp.int32, logits.shape, 1)
    logits = jnp.where(lane < N_EXPERTS, logits, -jnp.inf)
    e = jnp.exp(logits - jnp.max(logits, axis=-1, keepdims=True))
    p_ref[0] = e / jnp.sum(e, axis=-1, keepdims=True)
    o_ref[0] = xh


def modulate_router(x, g, shift, scale, w_router_padded):
    nb, t, _ = x.shape
    tm = min(t, 512)
    return pl.pallas_call(
        _modulate_router_kernel,
        grid=(nb, t // tm),
        in_specs=[
            pl.BlockSpec((1, tm, D_MODEL), lambda b, i: (b, i, 0)),
            pl.BlockSpec((1, D_MODEL), lambda b, i: (0, 0)),
            _row_vec_spec(shift, nb),
            _row_vec_spec(scale, nb),
            pl.BlockSpec((D_MODEL, LANES), lambda b, i: (0, 0)),
        ],
        out_specs=[
            pl.BlockSpec((1, tm, D_MODEL), lambda b, i: (b, i, 0)),
            pl.BlockSpec((1, tm, LANES), lambda b, i: (b, i, 0)),
        ],
        out_shape=[
            jax.ShapeDtypeStruct(x.shape, BF16),
            jax.ShapeDtypeStruct((nb, t, LANES), F32),
        ],
        compiler_params=_params("arbitrary", "arbitrary"),
        name="modulate_router",
    )(x, g, shift, scale, w_router_padded)


def _matmul_kernel(*refs, n_a, n_extra, n_out, epilogue):
    a_refs = refs[:n_a]
    w_ref = refs[n_a]
    extra = refs[n_a + 1:n_a + 1 + n_extra]
    outs = refs[n_a + 1 + n_extra:n_a + 1 + n_extra + n_out]
    wbf_ref = refs[-1]

    @pl.when(pl.program_id(1) == 0)
    def _():
        wbf_ref[...] = w_ref[...].astype(BF16)

    acc = None
    off = 0
    for a_ref in a_refs:
        k = a_ref.shape[-1]
        part = _dot(a_ref[...], wbf_ref[off:off + k, :])
        acc = part if acc is None else acc + part
        off += k
    epilogue(acc, extra, outs)


def fused_matmul(a_list, w, layer, col0, ncols, tm, tn, epilogue, extras, extra_specs, out_dtypes, name):
    m = a_list[0].shape[0]
    k_total = w.shape[1]
    assert sum(a.shape[1] for a in a_list) == k_total
    assert m % tm == 0 and ncols % tn == 0 and col0 % tn == 0
    jb = col0 // tn
    in_specs = [pl.BlockSpec((tm, a.shape[1]), lambda j, i: (i, 0)) for a in a_list]
    in_specs.append(pl.BlockSpec((None, k_total, tn), lambda j, i: (layer, 0, jb + j)))
    in_specs.extend(extra_specs)
    out_specs = [pl.BlockSpec((tm, tn), lambda j, i: (i, j)) for _ in out_dtypes]
    out_shape = [jax.ShapeDtypeStruct((m, ncols), dt) for dt in out_dtypes]
    kern = functools.partial(_matmul_kernel, n_a=len(a_list), n_extra=len(extras),
                             n_out=len(out_dtypes), epilogue=epilogue)
    return pl.pallas_call(
        kern,
        grid=(ncols // tn, m // tm),
        in_specs=in_specs,
        out_specs=out_specs,
        out_shape=out_shape,
        scratch_shapes=[pltpu.VMEM((k_total, tn), BF16)],
        compiler_params=_params("arbitrary", "arbitrary"),
        name=name,
    )(*a_list, w, *extras)


def _epi_cast(acc, extra, outs):
    outs[0][...] = acc.astype(outs[0].dtype)


def _epi_gelu(acc, extra, outs):
    outs[0][...] = (0.5 * acc * (1.0 + lax.erf(acc * (1.0 / math.sqrt(2.0))))).astype(outs[0].dtype)


def _epi_dft(acc, extra, outs):
    ch_ref, cl_ref, sh_ref, sl_ref = extra
    a_ref, b_ref = outs
    zh, zl = _split_bf16(acc)
    for g in range(acc.shape[1] // GROUP_DIM):
        sl = slice(g * GROUP_DIM, (g + 1) * GROUP_DIM)
        a_ref[:, sl] = (_dot(zh[:, sl], ch_ref[...]) + _dot(zl[:, sl], ch_ref[...])
                        + _dot(zh[:, sl], cl_ref[...])).astype(a_ref.dtype)
        b_ref[:, sl] = (_dot(zh[:, sl], sh_ref[...]) + _dot(zl[:, sl], sh_ref[...])
                        + _dot(zh[:, sl], sl_ref[...])).astype(b_ref.dtype)


def _epi_residual(acc, extra, outs):
    h_ref, gate_ref = extra
    outs[0][...] = h_ref[...] + gate_ref[0] * acc


def _residual_specs(h_flat, gate, t, tm, tn):
    per_batch = t // tm
    if gate.shape[0] == 1:
        gate_spec = pl.BlockSpec((1, 1, tn), lambda j, i: (0, 0, j))
    else:
        gate_spec = pl.BlockSpec((1, 1, tn), lambda j, i: (i // per_batch, 0, j))
    return [pl.BlockSpec((tm, tn), lambda j, i: (i, j)), gate_spec]


def _gmlp_kernel(u_ref, v_ref, vg_ref, ws_ref, bs_ref, o_ref):
    for c in range(v_ref.shape[1] // CHUNK):
        rows = slice(c * CHUNK, (c + 1) * CHUNK)
        v = v_ref[0, rows, :].astype(F32)
        vn = (v * lax.rsqrt(jnp.mean(v * v, axis=-1, keepdims=True) + EPS) * vg_ref[...]).astype(BF16)
        for g in range(A_GROUPS):
            sl = slice(g * GROUP_DIM, (g + 1) * GROUP_DIM)
            s = _dot(ws_ref[g].astype(BF16), vn[:, sl]) + bs_ref[:, g:g + 1]
            o_ref[0, rows, sl] = (u_ref[0, rows, sl].astype(F32) * s).astype(o_ref.dtype)


def chunk_gmlp(a, v_g, w_s, b_s_t, layer):
    nb, t, _ = a.shape
    rows = min(t, 512)
    return pl.pallas_call(
        _gmlp_kernel,
        grid=(nb, t // rows),
        in_specs=[
            pl.BlockSpec((1, rows, A_WIDTH), lambda b, c: (b, c, 0)),
            pl.BlockSpec((1, rows, A_WIDTH), lambda b, c: (b, c, 1)),
            pl.BlockSpec((None, 1, A_WIDTH), lambda b, c: (layer, 0, 0)),
            pl.BlockSpec((None, A_GROUPS, CHUNK, CHUNK), lambda b, c: (layer, 0, 0, 0)),
            pl.BlockSpec((None, CHUNK, A_GROUPS), lambda b, c: (layer, 0, 0)),
        ],
        out_specs=pl.BlockSpec((1, rows, A_WIDTH), lambda b, c: (b, c, 0)),
        out_shape=jax.ShapeDtypeStruct((nb, t, A_WIDTH), BF16),
        compiler_params=_params("arbitrary", "arbitrary"),
        name="chunk_gmlp",
    )(a, a, v_g.reshape(v_g.shape[0], 1, A_WIDTH), w_s, b_s_t)


def _dft_tables(n):
    k = np.arange(n, dtype=np.int64)
    ang = 2.0 * np.pi * ((k[:, None] * k[None, :]) % n).astype(np.float64) / n
    return np.cos(ang), np.sin(ang)


def _hi_lo(table):
    hi = jnp.asarray(table, dtype=F32).astype(BF16)
    lo = (jnp.asarray(table, dtype=F32) - hi.astype(F32)).astype(BF16)
    return hi, lo


def _token_dft_kernel(ct_ref, st_ref, a_ref, b_ref, o_ref, *, scale):
    y = _dot(ct_ref[...], a_ref[0]) - _dot(st_ref[...], b_ref[0])
    o_ref[0] = (y * scale).astype(o_ref.dtype)


def token_dft(a, b):
    nb, t, w = a.shape
    cos_t, sin_t = _dft_tables(t)
    ct = jnp.asarray(cos_t, dtype=F32).astype(BF16)
    st = jnp.asarray(sin_t, dtype=F32).astype(BF16)
    tm = min(t, 512)
    kern = functools.partial(_token_dft_kernel, scale=1.0 / math.sqrt(t * GROUP_DIM))
    return pl.pallas_call(
        kern,
        grid=(nb, t // tm),
        in_specs=[
            pl.BlockSpec((tm, t), lambda b_, i: (i, 0)),
            pl.BlockSpec((tm, t), lambda b_, i: (i, 0)),
            pl.BlockSpec((1, t, w), lambda b_, i: (b_, 0, 0)),
            pl.BlockSpec((1, t, w), lambda b_, i: (b_, 0, 0)),
        ],
        out_specs=pl.BlockSpec((1, tm, w), lambda b_, i: (b_, i, 0)),
        out_shape=jax.ShapeDtypeStruct((nb, t, w), BF16),
        compiler_params=_params("arbitrary", "arbitrary"),
        name="token_dft",
    )(ct, st, a, b)


def mixer_ab(xn, h, gate, ab_w_in, ab_w_out, gm_v_g, gm_w_s, gm_b_s_t, layer):
    nb, t, _ = xn.shape
    m = nb * t
    xf = xn.reshape(m, D_MODEL)
    tm = min(m, 512)
    (a,) = fused_matmul([xf], ab_w_in, layer, 0, 2 * A_WIDTH, tm, 1024, _epi_gelu, [], [], [BF16], "ab_in_gelu")
    c128, s128 = _dft_tables(GROUP_DIM)
    tables = [*_hi_lo(c128), *_hi_lo(s128)]
    table_specs = [pl.BlockSpec((GROUP_DIM, GROUP_DIM), lambda j, i: (0, 0)) for _ in tables]
    za, zb = fused_matmul([xf], ab_w_in, layer, 2 * A_WIDTH, B_WIDTH, tm, 1024, _epi_dft, tables, table_specs,
                          [BF16, BF16], "ab_in_dft")
    ya = chunk_gmlp(a.reshape(nb, t, 2 * A_WIDTH), gm_v_g, gm_w_s, gm_b_s_t, layer)
    yb = token_dft(za.reshape(nb, t, B_WIDTH), zb.reshape(nb, t, B_WIDTH))
    hf = h.reshape(m, D_MODEL)
    tn = 1024
    (out,) = fused_matmul([ya.reshape(m, A_WIDTH), yb.reshape(m, B_WIDTH)], ab_w_out, layer, 0, D_MODEL, tm, tn,
                          _epi_residual, [hf, gate], _residual_specs(hf, gate, t, tm, tn), [F32], "ab_out")
    return out.reshape(nb, t, D_MODEL)


def _conv_kernel(prev_ref, main_ref, next_ref, w_ref, b_ref, o_ref, *, rows):
    r = pl.program_id(1)
    last = pl.num_programs(1) - 1
    prev = jnp.where(r > 0, prev_ref[0].astype(F32), 0.0)
    nxt = jnp.where(r < last, next_ref[0].astype(F32), 0.0)
    full = jnp.concatenate([prev, main_ref[0].astype(F32), nxt], axis=0)
    n = full.shape[0]
    w = w_ref[...]
    acc = None
    for k in range(D_CONV):
        shift = (D_CONV // 2 - k) % n
        term = (full if shift == 0 else pltpu.roll(full, shift, 0)) * w[k:k + 1, :]
        acc = term if acc is None else acc + term
    y = acc[BF16_SUBLANES:BF16_SUBLANES + rows, :] + b_ref[...]
    o_ref[0] = _silu(y).astype(o_ref.dtype)


def dwconv_silu(z, conv_w, conv_b, layer):
    nb, t, ch = z.shape
    rows = min(t, 512)
    tc = 1024
    halo = BF16_SUBLANES
    per = rows // halo
    n_halo = t // halo
    kern = functools.partial(_conv_kernel, rows=rows)
    return pl.pallas_call(
        kern,
        grid=(nb, t // rows, ch // tc),
        in_specs=[
            pl.BlockSpec((1, halo, tc), lambda b, r, c: (b, jnp.maximum(r * per - 1, 0), c)),
            pl.BlockSpec((1, rows, tc), lambda b, r, c: (b, r, c)),
            pl.BlockSpec((1, halo, tc), lambda b, r, c: (b, jnp.minimum((r + 1) * per, n_halo - 1), c)),
            pl.BlockSpec((None, D_CONV, tc), lambda b, r, c: (layer, 0, c)),
            pl.BlockSpec((None, 1, tc), lambda b, r, c: (layer, 0, c)),
        ],
        out_specs=pl.BlockSpec((1, rows, tc), lambda b, r, c: (b, r, c)),
        out_shape=jax.ShapeDtypeStruct(z.shape, BF16),
        compiler_params=_params("arbitrary", "arbitrary", "arbitrary"),
        name="dwconv_silu",
    )(z, z, z, conv_w, conv_b.reshape(conv_b.shape[0], 1, conv_b.shape[1]))


def _softplus(x):
    return jnp.maximum(x, 0.0) + jnp.log(1.0 + jnp.exp(-jnp.abs(x)))


def _prefix_sum(x, axis):
    n = x.shape[axis]
    idx = lax.broadcasted_iota(jnp.int32, x.shape, axis)
    s = 1
    while s < n:
        x = x + jnp.where(idx >= s, pltpu.roll(x, s, axis), 0.0)
        s *= 2
    return x


def _ssd_prep_kernel(raw_ref, bias_ref, a_ref, cumc_ref, cumr_ref, dtr_ref, wr_ref):
    L = SSD_CHUNK
    dt = _softplus(raw_ref[0] + bias_ref[...])
    la = dt * (a_ref[...] * LOG2E)
    cum = _prefix_sum(la, 0)
    tot = cum[L - 1:L, :]
    lane = lax.broadcasted_iota(jnp.int32, cum.shape, 1)
    is_bwd = (lane % (2 * HEADS_PER_GROUP)) >= HEADS_PER_GROUP
    cum = jnp.where(is_bwd, tot - cum + la, cum)
    w = jnp.exp2(tot - cum) * dt
    cum_t = cum.T
    dt_t = dt.T
    w_t = w.T
    gw = 2 * HEADS_PER_GROUP
    for g in range(N_BC_GROUPS):
        ls = slice(g * gw, (g + 1) * gw)
        cumc_ref[0, g] = cum[:, ls]
        cumr_ref[0, g] = cum_t[ls, :]
        dtr_ref[0, g] = dt_t[ls, :]
        wr_ref[0, g] = w_t[ls, :]


def ssd_prep(dt_raw, bias, a):
    nb, t, n = dt_raw.shape
    g, gw, L = N_BC_GROUPS, 2 * HEADS_PER_GROUP, SSD_CHUNK
    col = jax.ShapeDtypeStruct((nb, g, t, gw), F32)
    row = jax.ShapeDtypeStruct((nb, g, gw, t), F32)
    col_spec = pl.BlockSpec((1, g, L, gw), lambda b, c: (b, 0, c, 0))
    row_spec = pl.BlockSpec((1, g, gw, L), lambda b, c: (b, 0, 0, c))
    return pl.pallas_call(
        _ssd_prep_kernel,
        grid=(nb, t // L),
        in_specs=[
            pl.BlockSpec((1, L, n), lambda b, c: (b, c, 0)),
            pl.BlockSpec((1, n), lambda b, c: (0, 0)),
            pl.BlockSpec((1, n), lambda b, c: (0, 0)),
        ],
        out_specs=[col_spec, row_spec, row_spec, row_spec],
        out_shape=[col, row, row, row],
        compiler_params=_params("arbitrary", "arbitrary"),
        name="ssd_prep",
    )(dt_raw, bias, a)


def _ssd_chunk(c, direction, with_y, x_ref, b_ref, c_ref, cumc_ref, cumr_ref, dtr_ref, wr_ref, y_ref, state_ref):
    L = SSD_CHUNK
    hp = HEADS_PER_GROUP
    h0 = direction * hp
    rows = pl.ds(pl.multiple_of(c * L, L), L)
    lane_lo = lax.broadcasted_iota(jnp.int32, (L, 2 * HEADDIM), 1) < HEADDIM
    row_lo = lane_lo[:1]
    edge = L - 1 if direction == 0 else 0

    bm = b_ref[0, rows, :]
    bm_t = bm.astype(F32).T
    xc = x_ref[0, rows, :]
    cumr = cumr_ref[0, 0, :, rows]
    wr = wr_ref[0, 0, :, rows]
    s_prev = state_ref[direction]

    if with_y:
        cm = c_ref[0, rows, :]
        cb = lax.dot_general(cm, bm, (((1,), (1,)), ((), ())), preferred_element_type=F32)
        ii = lax.broadcasted_iota(jnp.int32, (L, L), 0)
        jj = lax.broadcasted_iota(jnp.int32, (L, L), 1)
        causal = (jj <= ii) if direction == 0 else (jj >= ii)
        cbm = jnp.where(causal, cb, 0.0)
        cumc = cumc_ref[0, 0, rows, :]
        dtr = dtr_ref[0, 0, :, rows]
        y_inter = _dot(cm, s_prev.astype(BF16))

    ys, states = [], []
    for p in range(hp // 2):
        k1, k2 = h0 + 2 * p, h0 + 2 * p + 1
        cols = slice(p * 2 * HEADDIM, (p + 1) * 2 * HEADDIM)
        xp = xc[:, cols]
        zero = jnp.zeros_like(xp)
        rhs2 = jnp.concatenate([jnp.where(lane_lo, xp, zero), jnp.where(lane_lo, zero, xp)], axis=0)
        if with_y:
            ms, cum_b = [], []
            for k in (k1, k2):
                cum_b.append(jnp.broadcast_to(cumc[:, k:k + 1], (L, L)))
                seg = jnp.minimum(cum_b[-1] - cumr[k:k + 1, :], 0.0)
                ms.append((cbm * jnp.exp2(seg) * dtr[k:k + 1, :]).astype(BF16))
            edec = jnp.exp2(jnp.where(lane_lo, cum_b[0], cum_b[1]))
            ys.append(_dot(jnp.concatenate(ms, axis=1), rhs2) + y_inter[:, cols] * edec)
        bw = jnp.concatenate([(bm_t * wr[k1:k1 + 1, :]).astype(BF16), (bm_t * wr[k2:k2 + 1, :]).astype(BF16)], axis=1)
        d1 = jnp.exp2(cumr[k1:k1 + 1, edge:edge + 1])
        d2 = jnp.exp2(cumr[k2:k2 + 1, edge:edge + 1])
        dec = jnp.where(row_lo, jnp.broadcast_to(d1, row_lo.shape), jnp.broadcast_to(d2, row_lo.shape))
        states.append(s_prev[:, cols] * dec + _dot(bw, rhs2))
    state_ref[direction] = jnp.concatenate(states, axis=1)
    if with_y:
        y_ref[direction, rows, :] = jnp.concatenate(ys, axis=1)


def _ssd_kernel(*refs, n_chunks, with_y):
    if with_y:
        (x_ref, b_ref, c_ref, z_ref, cumc_ref, cumr_ref, dtr_ref, wr_ref, dskip_ref, s0_ref,
         y_out_ref, sfin_ref, state_ref, y_ref) = refs
    else:
        x_ref, b_ref, cumr_ref, wr_ref, s0_ref, sfin_ref, state_ref = refs
        c_ref = z_ref = cumc_ref = dtr_ref = y_ref = None
    state_ref[...] = s0_ref[:, 0, 0]
    args = (x_ref, b_ref, c_ref, cumc_ref, cumr_ref, dtr_ref, wr_ref, y_ref, state_ref)

    def both(c, carry):
        _ssd_chunk(c, 0, with_y, *args)
        _ssd_chunk(n_chunks - 1 - c, 1, with_y, *args)
        return carry

    lax.fori_loop(0, n_chunks, both, 0)
    sfin_ref[:, 0, 0] = state_ref[...]

    if with_y:
        def gate(c, carry):
            rows = pl.ds(pl.multiple_of(c * SSD_CHUNK, SSD_CHUNK), SSD_CHUNK)
            y = y_ref[0, rows, :] + y_ref[1, rows, :] + x_ref[0, rows, :].astype(F32) * dskip_ref[0]
            y_out_ref[0, rows, :] = (y * _silu(z_ref[0, rows, :].astype(F32))).astype(y_out_ref.dtype)
            return carry

        lax.fori_loop(0, n_chunks, gate, 0)


def ssd_scan(xbc, prep, d_skip, s0, z=None):
    nb, t, _ = xbc.shape
    g, gw = N_BC_GROUPS, 2 * HEADS_PER_GROUP
    cumc, cumr, dtr, wr = prep
    with_y = z is not None
    b_off = D_INNER // D_STATE
    c_off = (D_INNER + GN) // D_STATE
    x_spec = pl.BlockSpec((1, t, GROUP_WIDTH), lambda b, gi: (b, 0, gi))
    b_spec = pl.BlockSpec((1, t, D_STATE), lambda b, gi: (b, 0, b_off + gi))
    c_spec = pl.BlockSpec((1, t, D_STATE), lambda b, gi: (b, 0, c_off + gi))
    col_spec = pl.BlockSpec((1, 1, t, gw), lambda b, gi: (b, gi, 0, 0))
    row_spec = pl.BlockSpec((1, 1, gw, t), lambda b, gi: (b, gi, 0, 0))
    s_spec = pl.BlockSpec((2, 1, 1, D_STATE, GROUP_WIDTH), lambda b, gi: (0, b, gi, 0, 0))
    s_shape = jax.ShapeDtypeStruct(s0.shape, F32)
    state = pltpu.VMEM((2, D_STATE, GROUP_WIDTH), F32)
    kern = functools.partial(_ssd_kernel, n_chunks=t // SSD_CHUNK, with_y=with_y)
    if not with_y:
        return pl.pallas_call(
            kern,
            grid=(nb, g),
            in_specs=[x_spec, b_spec, row_spec, row_spec, s_spec],
            out_specs=s_spec,
            out_shape=s_shape,
            scratch_shapes=[state],
            compiler_params=_params("arbitrary", "arbitrary"),
            name="ssd_states",
        )(xbc, xbc, cumr, wr, s0)
    dskip = jnp.repeat(d_skip, HEADDIM).reshape(g, 1, GROUP_WIDTH)
    return pl.pallas_call(
        kern,
        grid=(nb, g),
        in_specs=[x_spec, b_spec, c_spec, x_spec, col_spec, row_spec, row_spec, row_spec,
                  pl.BlockSpec((1, 1, GROUP_WIDTH), lambda b, gi: (gi, 0, 0)), s_spec],
        out_specs=[x_spec, s_spec],
        out_shape=[jax.ShapeDtypeStruct((nb, t, D_INNER), BF16), s_shape],
        scratch_shapes=[state, pltpu.VMEM((2, t, GROUP_WIDTH), F32)],
        compiler_params=_params("arbitrary", "arbitrary"),
        name="ssd_scan",
    )(xbc, xbc, xbc, z, cumc, cumr, dtr, wr, dskip, s0)


def _rms_scale_kernel(y_ref, g_ref, o_ref):
    y = y_ref[...].astype(F32)
    o_ref[...] = (y * lax.rsqrt(jnp.mean(y * y, axis=-1, keepdims=True) + EPS) * g_ref[...]).astype(o_ref.dtype)


def rms_scale(y, norm_g, layer):
    m = y.shape[0]
    tm = min(m, 512)
    return pl.pallas_call(
        _rms_scale_kernel,
        grid=(m // tm,),
        in_specs=[
            pl.BlockSpec((tm, D_INNER), lambda i: (i, 0)),
            pl.BlockSpec((None, 1, D_INNER), lambda i: (layer, 0, 0)),
        ],
        out_specs=pl.BlockSpec((tm, D_INNER), lambda i: (i, 0)),
        out_shape=jax.ShapeDtypeStruct((m, D_INNER), BF16),
        compiler_params=_params("arbitrary"),
        name="rms_scale",
    )(y, norm_g.reshape(norm_g.shape[0], 1, D_INNER))


def _group_major(v):
    return v.reshape(2, N_BC_GROUPS, HEADS_PER_GROUP).transpose(1, 0, 2).reshape(1, 2 * N_HEADS)


def ssd_project(xn, w_in, w_dt, conv_w, conv_b, dt_bias, a, layer, with_z):
    nb, t, _ = xn.shape
    m = nb * t
    xf = xn.reshape(m, D_MODEL)
    tm = min(m, 512)
    conv_cols = CONV_DIM if with_z else D_INNER + GN
    (xbc,) = fused_matmul([xf], w_in, layer, D_INNER, conv_cols, tm, 1024, _epi_cast, [], [], [BF16], "ssd_in_xbc")
    (dt_raw,) = fused_matmul([xf], w_dt, 0, 0, 2 * N_HEADS, tm, 2 * N_HEADS, _epi_cast, [], [], [F32], "ssd_in_dt")
    xbc = dwconv_silu(xbc.reshape(nb, t, conv_cols), conv_w, conv_b, layer)
    prep = ssd_prep(dt_raw.reshape(nb, t, 2 * N_HEADS), _group_major(dt_bias), _group_major(a))
    z = None
    if with_z:
        (z,) = fused_matmul([xf], w_in, layer, 0, D_INNER, tm, 1024, _epi_cast, [], [], [BF16], "ssd_in_z")
        z = z.reshape(nb, t, D_INNER)
    return xbc, prep, z


def ssd_mix(xn, xc, h, gate, w_in, conv_w, conv_b, dt_bias, a_log, d_skip, norm_g, w_out, layer):
    nb, t, _ = xn.shape
    m = nb * t
    a = -jnp.exp(a_log[layer].astype(F32))
    w_dt = w_in[layer, :, D_INNER + CONV_DIM:].reshape(D_MODEL, 2, N_BC_GROUPS, HEADS_PER_GROUP)
    w_dt = w_dt.transpose(0, 2, 1, 3).reshape(1, D_MODEL, 2 * N_HEADS)
    zero = jnp.zeros((2, nb, N_BC_GROUPS, D_STATE, GROUP_WIDTH), F32)
    xbc_c, prep_c, _ = ssd_project(xc, w_in, w_dt, conv_w, conv_b, dt_bias[layer], a, layer, False)
    s_ctx = ssd_scan(xbc_c, prep_c, d_skip[layer], zero)
    xbc, prep, z = ssd_project(xn, w_in, w_dt, conv_w, conv_b, dt_bias[layer], a, layer, True)
    y, _ = ssd_scan(xbc, prep, d_skip[layer], s_ctx, z)
    yn = rms_scale(y.reshape(m, D_INNER), norm_g, layer)
    hf = h.reshape(m, D_MODEL)
    tm, tn = 512, 512
    (out,) = fused_matmul([yn], w_out, layer, 0, D_MODEL, tm, tn, _epi_residual, [hf, gate],
                          _residual_specs(hf, gate, t, tm, tn), [F32], "ssd_out")
    return out.reshape(nb, t, D_MODEL)


def _route_kernel(p_ref, tri_ref, pos_ref, post_ref, gate_ref, *, cap):
    probs = p_ref[0]
    t = probs.shape[0]
    pe = probs.T[:N_EXPERTS, :]
    bits = pltpu.bitcast(pe, jnp.int32)

    def step(i, v):
        cand = v | jnp.left_shift(jnp.int32(1), 30 - i)
        cnt = jnp.sum((bits >= cand).astype(F32), axis=1, keepdims=True)
        return jnp.where(cnt >= cap, cand, v)

    thr = lax.fori_loop(0, 31, step, jnp.zeros((N_EXPERTS, 1), jnp.int32))
    gt = bits > thr
    eq = bits == thr
    need = cap - jnp.sum(gt.astype(F32), axis=1, keepdims=True)
    tri = tri_ref[...]
    eq_rank = _dot(eq.astype(BF16), tri)
    sel = gt | (eq & (eq_rank <= need))
    pos = _dot(sel.astype(BF16), tri) - 1.0
    pos = jnp.where(sel, pos, -1.0)
    pos_ref[0] = pos
    pos_full = jnp.concatenate([pos, jnp.full((LANES - N_EXPERTS, t), -1.0, F32)], axis=0)
    pos_t = pos_full.T
    post_ref[0] = pos_t
    gate = jnp.where(pos_t >= 0.0, probs, 0.0)
    hi = gate.astype(BF16).astype(F32)
    mid = (gate - hi).astype(BF16).astype(F32)
    lo = (gate - hi - mid).astype(BF16).astype(F32)
    gate_ref[0] = (hi + pltpu.roll(mid, N_EXPERTS, 1) + pltpu.roll(lo, 2 * N_EXPERTS, 1)).astype(BF16)


def route(probs, cap):
    nb, t, _ = probs.shape
    idx = np.arange(t)
    tri = jnp.asarray((idx[:, None] <= idx[None, :]).astype(np.float32), dtype=BF16)
    kern = functools.partial(_route_kernel, cap=cap)
    return pl.pallas_call(
        kern,
        grid=(nb,),
        in_specs=[
            pl.BlockSpec((1, t, LANES), lambda b: (b, 0, 0)),
            pl.BlockSpec((t, t), lambda b: (0, 0)),
        ],
        out_specs=[
            pl.BlockSpec((1, N_EXPERTS, t), lambda b: (b, 0, 0)),
            pl.BlockSpec((1, t, LANES), lambda b: (b, 0, 0)),
            pl.BlockSpec((1, t, LANES), lambda b: (b, 0, 0)),
        ],
        out_shape=[
            jax.ShapeDtypeStruct((nb, N_EXPERTS, t), F32),
            jax.ShapeDtypeStruct((nb, t, LANES), F32),
            jax.ShapeDtypeStruct((nb, t, LANES), BF16),
        ],
        compiler_params=_params("arbitrary"),
        name="route",
    )(probs, tri)


def _gather_kernel(pos_ref, x_ref, gate_ref, o_ref, gs_ref, *, cap):
    e = pl.program_id(1)
    pos = pos_ref[0, 0]
    t = pos.shape[1]
    slot = lax.broadcasted_iota(jnp.int32, (cap, t), 0).astype(F32)
    onehot = (slot == pos).astype(BF16)
    o_ref[0] = _dot(onehot, x_ref[0]).astype(o_ref.dtype)
    pieces = _dot(onehot, gate_ref[0])
    lane = lax.broadcasted_iota(jnp.int32, pieces.shape, 1)
    mine = (lane < 3 * N_EXPERTS) & (lane % N_EXPERTS == e)
    gs_ref[0] = jnp.sum(jnp.where(mine, pieces, 0.0), axis=1, keepdims=True)


def gather_tokens(pos, xm, gate_split, cap):
    nb, t, _ = xm.shape
    kern = functools.partial(_gather_kernel, cap=cap)
    return pl.pallas_call(
        kern,
        grid=(nb, N_EXPERTS),
        in_specs=[
            pl.BlockSpec((1, 1, 1, t), lambda b, e: (b, e, 0, 0)),
            pl.BlockSpec((1, t, D_MODEL), lambda b, e: (b, 0, 0)),
            pl.BlockSpec((1, t, LANES), lambda b, e: (b, 0, 0)),
        ],
        out_specs=[
            pl.BlockSpec((1, cap, D_MODEL), lambda b, e: (e, b, 0)),
            pl.BlockSpec((1, cap, 1), lambda b, e: (e, b, 0)),
        ],
        out_shape=[
            jax.ShapeDtypeStruct((N_EXPERTS, nb * cap, D_MODEL), BF16),
            jax.ShapeDtypeStruct((N_EXPERTS, nb * cap, 1), F32),
        ],
        compiler_params=_params("arbitrary", "arbitrary"),
        name="gather_tokens",
    )(pos.reshape(nb, N_EXPERTS, 1, t), xm, gate_split)


def _ffn_kernel(x_ref, gs_ref, wg_ref, wu_ref, wd_ref, o_ref, acc_ref):
    f = pl.program_id(2)
    x = x_ref[0]
    gate = _dot(x, wg_ref[...].astype(BF16))
    up = _dot(x, wu_ref[...].astype(BF16))
    hid = (_silu(gate) * up).astype(BF16)

    @pl.when(f == 0)
    def _():
        acc_ref[...] = jnp.zeros_like(acc_ref)

    acc_ref[...] += _dot(hid, wd_ref[...].astype(BF16))

    @pl.when(f == pl.num_programs(2) - 1)
    def _():
        o_ref[0] = (acc_ref[...] * gs_ref[0]).astype(o_ref.dtype)


def expert_ffn(xe, gate_slot, w_gate, w_up, w_down, layer):
    _, m, _ = xe.shape
    tm = min(m, 1024)
    tf = 256
    return pl.pallas_call(
        _ffn_kernel,
        grid=(N_EXPERTS, m // tm, D_MODEL // tf),
        in_specs=[
            pl.BlockSpec((1, tm, D_MODEL), lambda e, i, f: (e, i, 0)),
            pl.BlockSpec((1, tm, 1), lambda e, i, f: (e, i, 0)),
            pl.BlockSpec((None, None, D_MODEL, tf), lambda e, i, f: (layer, e, 0, f)),
            pl.BlockSpec((None, None, D_MODEL, tf), lambda e, i, f: (layer, e, 0, f)),
            pl.BlockSpec((None, None, tf, D_MODEL), lambda e, i, f: (layer, e, f, 0)),
        ],
        out_specs=pl.BlockSpec((1, tm, D_MODEL), lambda e, i, f: (e, i, 0)),
        out_shape=jax.ShapeDtypeStruct(xe.shape, BF16),
        scratch_shapes=[pltpu.VMEM((tm, D_MODEL), F32)],
        compiler_params=_params("arbitrary", "arbitrary", "arbitrary"),
        name="expert_ffn",
    )(xe, gate_slot, w_gate, w_up, w_down)


def _combine_kernel(post_ref, y_ref, h_ref, g2_ref, o_ref, *, cap):
    tq = post_ref.shape[1]
    slot = lax.broadcasted_iota(jnp.int32, (tq, cap), 1).astype(F32)
    pos_t = post_ref[0]
    acc = None
    for e in range(N_EXPERTS):
        onehot = (slot == pos_t[:, e:e + 1]).astype(BF16)
        part = _dot(onehot, y_ref[e])
        acc = part if acc is None else acc + part
    o_ref[0] = h_ref[0] + g2_ref[0] * acc


def combine(pos_t, ye, h, g2, cap):
    nb, t, _ = h.shape
    tq = min(t, 256)
    if g2.shape[0] == 1:
        g2_spec = pl.BlockSpec((1, 1, D_MODEL), lambda b, i: (0, 0, 0))
    else:
        g2_spec = pl.BlockSpec((1, 1, D_MODEL), lambda b, i: (b, 0, 0))
    kern = functools.partial(_combine_kernel, cap=cap)
    return pl.pallas_call(
        kern,
        grid=(nb, t // tq),
        in_specs=[
            pl.BlockSpec((1, tq, LANES), lambda b, i: (b, i, 0)),
            pl.BlockSpec((N_EXPERTS, cap, D_MODEL), lambda b, i: (0, b, 0)),
            pl.BlockSpec((1, tq, D_MODEL), lambda b, i: (b, i, 0)),
            g2_spec,
        ],
        out_specs=pl.BlockSpec((1, tq, D_MODEL), lambda b, i: (b, i, 0)),
        out_shape=jax.ShapeDtypeStruct(h.shape, F32),
        compiler_params=_params("arbitrary", "arbitrary"),
        name="combine",
    )(pos_t, ye, h, g2)


def expert_choice_ffn(h, g, shift, scale, g2, w_router, w_gate, w_up, w_down, layer):
    nb, t, _ = h.shape
    cap = CAPACITY_FACTOR * t // N_EXPERTS
    wr = jnp.pad(w_router[layer], ((0, 0), (0, LANES - N_EXPERTS)))
    xm, probs = modulate_router(h, g, shift, scale, wr)
    pos, pos_t, gate_split = route(probs, cap)
    xe, gate_slot = gather_tokens(pos, xm, gate_split, cap)
    ye = expert_ffn(xe, gate_slot, w_gate, w_up, w_down, layer)
    return combine(pos_t, ye, h, g2, cap)


def _rmsnorm_kernel(x_ref, g_ref, o_ref):
    x = x_ref[...]
    o_ref[...] = x * lax.rsqrt(jnp.mean(x * x, axis=-1, keepdims=True) + EPS) * g_ref[...]


def rmsnorm(x, g):
    m = x.shape[0]
    tm = min(m, 512)
    return pl.pallas_call(
        _rmsnorm_kernel,
        grid=(m // tm,),
        in_specs=[pl.BlockSpec((tm, D_MODEL), lambda i: (i, 0)), pl.BlockSpec((1, D_MODEL), lambda i: (0, 0))],
        out_specs=pl.BlockSpec((tm, D_MODEL), lambda i: (i, 0)),
        out_shape=jax.ShapeDtypeStruct(x.shape, F32),
        compiler_params=_params("arbitrary"),
        name="final_rmsnorm",
    )(x, g)


def _sincos_2d(rows, cols, dim):
    quarter = dim // 4
    omega = 1.0 / (10000.0 ** (jnp.arange(quarter, dtype=F32) / quarter))
    r = jnp.repeat(jnp.arange(rows, dtype=F32), cols)[:, None] * omega
    cl = jnp.tile(jnp.arange(cols, dtype=F32), rows)[:, None] * omega
    return jnp.concatenate([jnp.sin(r), jnp.cos(r), jnp.sin(cl), jnp.cos(cl)], axis=-1)


def kernel(x, c, ctx, c_ctx, mod_w, mod_b, norm_g, final_g, ab_w_in, ab_w_out, gm_v_g, gm_w_s, gm_b_s,
           ssd_w_in, ssd_conv_w, ssd_conv_b, ssd_dt_bias, ssd_a_log, ssd_d, ssd_norm_g, ssd_w_out,
           moe_w_router, moe_w_gate, moe_w_up, moe_w_down):
    nb, n, _ = x.shape
    depth = mod_w.shape[0]
    h = x + _sincos_2d(n // GRID_W, GRID_W, D_MODEL)[None]
    hc = ctx
    cond = jnp.concatenate([c, c_ctx[None], jnp.zeros((16 - nb - 1, D_MODEL), F32)], axis=0)
    gm_b_s_t = jnp.swapaxes(gm_b_s, 1, 2)
    for i in range(depth):
        last = i == depth - 1
        even = i % 2 == 0
        j = i // 2
        ctx_reaches_latent = (not last) or (not even)
        mod = adaln(cond, mod_w, mod_b, i).reshape(16, 6, 1, D_MODEL)
        sh1, sc1, g1, sh2, sc2, g2 = (mod[:nb, k] for k in range(6))
        csh1, csc1, cg1, csh2, csc2, cg2 = (mod[nb:nb + 1, k] for k in range(6))
        g_mix = norm_g[i, 0][None]
        g_ffn = norm_g[i, 1][None]
        xn = modulate(h, g_mix, sh1, sc1)
        if ctx_reaches_latent:
            xc = modulate(hc, g_mix, csh1, csc1)
        if even:
            h = mixer_ab(xn, h, g1, ab_w_in, ab_w_out, gm_v_g, gm_w_s, gm_b_s_t, j)
            if not last:
                hc = mixer_ab(xc, hc, cg1, ab_w_in, ab_w_out, gm_v_g, gm_w_s, gm_b_s_t, j)
        else:
            assert last, "context outputs of a state-space layer are only needed before the last layer"
            h = ssd_mix(xn, xc, h, g1, ssd_w_in, ssd_conv_w, ssd_conv_b, ssd_dt_bias, ssd_a_log, ssd_d,
                        ssd_norm_g, ssd_w_out, j)
        h = expert_choice_ffn(h, g_ffn, sh2, sc2, g2, moe_w_router, moe_w_gate, moe_w_up, moe_w_down, i)
        if not last:
            hc = expert_choice_ffn(hc, g_ffn, csh2, csc2, cg2, moe_w_router, moe_w_gate, moe_w_up, moe_w_down, i)
    return rmsnorm(h.reshape(nb * n, D_MODEL), final_g[None]).reshape(nb, n, D_MODEL)
```

```python
import functools
import math

import jax
import jax.numpy as jnp
import numpy as np
from jax import lax
from jax.experimental import pallas as pl
from jax.experimental.pallas import tpu as pltpu

F32 = jnp.float32
BF16 = jnp.bfloat16

D_MODEL = 2048
GRID_W = 64
EPS = 1e-6

A_WIDTH = D_MODEL // 2
GROUP_DIM = 128
A_GROUPS = A_WIDTH // GROUP_DIM
CHUNK = 128
B_WIDTH = D_MODEL // 2

D_INNER = 2 * D_MODEL
HEADDIM = 64
N_HEADS = D_INNER // HEADDIM
D_STATE = 128
N_BC_GROUPS = 8
HEADS_PER_GROUP = N_HEADS // N_BC_GROUPS
D_CONV = 5
SSD_CHUNK = 128
GN = N_BC_GROUPS * D_STATE
CONV_DIM = D_INNER + 2 * GN
GROUP_WIDTH = HEADS_PER_GROUP * HEADDIM

N_EXPERTS = 16
CAPACITY_FACTOR = 2

LANES = 128
BF16_SUBLANES = 16
VMEM_LIMIT = 56 * 1024 * 1024
LOG2E = 1.4426950408889634


def _params(*sem):
    return pltpu.CompilerParams(dimension_semantics=sem, vmem_limit_bytes=VMEM_LIMIT)


def _silu(x):
    return x / (1.0 + jnp.exp(-x))


def _split_bf16(x):
    hi = x.astype(BF16)
    lo = (x - hi.astype(F32)).astype(BF16)
    return hi, lo


def _dot(a, b):
    return jnp.dot(a, b, preferred_element_type=F32)


def _adaln_kernel(c_ref, w_ref, b_ref, o_ref):
    a = _silu(c_ref[...]).astype(BF16)
    o_ref[...] = _dot(a, w_ref[...].astype(BF16)) + b_ref[...]


def adaln(cond, mod_w, mod_b, layer):
    rows = cond.shape[0]
    tn = 1024
    n = 6 * D_MODEL
    return pl.pallas_call(
        _adaln_kernel,
        grid=(n // tn,),
        in_specs=[
            pl.BlockSpec((rows, D_MODEL), lambda j: (0, 0)),
            pl.BlockSpec((None, D_MODEL, tn), lambda j: (layer, 0, j)),
            pl.BlockSpec((None, 1, tn), lambda j: (layer, 0, j)),
        ],
        out_specs=pl.BlockSpec((rows, tn), lambda j: (0, j)),
        out_shape=jax.ShapeDtypeStruct((rows, n), F32),
        compiler_params=_params("arbitrary"),
        name="adaln",
    )(cond, mod_w, mod_b.reshape(mod_b.shape[0], 1, n))


def _rms_modulate(x, g, shift, scale):
    ms = jnp.mean(x * x, axis=-1, keepdims=True)
    y = x * lax.rsqrt(ms + EPS) * g
    return y * (1.0 + scale) + shift


def _modulate_kernel(*refs, with_pos):
    if with_pos:
        x_ref, pos_ref, g_ref, sh_ref, sc_ref, o_ref = refs
        x = x_ref[0] + pos_ref[...]
    else:
        x_ref, g_ref, sh_ref, sc_ref, o_ref = refs
        x = x_ref[0]
    o_ref[0] = _rms_modulate(x, g_ref[...], sh_ref[0], sc_ref[0]).astype(o_ref.dtype)


def _row_vec_spec(vec, n_batch):
    if vec.shape[0] == 1:
        return pl.BlockSpec((1, 1, D_MODEL), lambda b, i: (0, 0, 0))
    assert vec.shape[0] == n_batch
    return pl.BlockSpec((1, 1, D_MODEL), lambda b, i: (b, 0, 0))


def modulate(x, g, shift, scale, pos=None):
    nb, t, _ = x.shape
    tm = min(t, 512)
    in_specs = [pl.BlockSpec((1, tm, D_MODEL), lambda b, i: (b, i, 0))]
    args = [x]
    if pos is not None:
        in_specs.append(pl.BlockSpec((tm, D_MODEL), lambda b, i: (i, 0)))
        args.append(pos)
    in_specs += [pl.BlockSpec((1, D_MODEL), lambda b, i: (0, 0)), _row_vec_spec(shift, nb), _row_vec_spec(scale, nb)]
    return pl.pallas_call(
        functools.partial(_modulate_kernel, with_pos=pos is not None),
        grid=(nb, t // tm),
        in_specs=in_specs,
        out_specs=pl.BlockSpec((1, tm, D_MODEL), lambda b, i: (b, i, 0)),
        out_shape=jax.ShapeDtypeStruct(x.shape, BF16),
        compiler_params=_params("arbitrary", "arbitrary"),
        name="modulate",
    )(*args, g, shift, scale)


def _modulate_router_kernel(x_ref, g_ref, sh_ref, sc_ref, wr_ref, o_ref, p_ref):
    xm = _rms_modulate(x_ref[0], g_ref[...], sh_ref[0], sc_ref[0])
    xh, xl = _split_bf16(xm)
    wh, wl = _split_bf16(wr_ref[...])
    logits = _dot(xh, wh) + _dot(xl, wh) + _dot(xh, wl)
    lane = lax.broadcasted_iota(jnp.int32, logits.shape, 1)
    logits = jnp.where(lane < N_EXPERTS, logits, -jnp.inf)
    e = jnp.exp(logits - jnp.max(logits, axis=-1, keepdims=True))
    p_ref[0] = e / jnp.sum(e, axis=-1, keepdims=True)
    o_ref[0] = xh


def modulate_router(x, g, shift, scale, w_router_padded):
    nb, t, _ = x.shape
    tm = min(t, 512)
    return pl.pallas_call(
        _modulate_router_kernel,
        grid=(nb, t // tm),
        in_specs=[
            pl.BlockSpec((1, tm, D_MODEL), lambda b, i: (b, i, 0)),
            pl.BlockSpec((1, D_MODEL), lambda b, i: (0, 0)),
            _row_vec_spec(shift, nb),
            _row_vec_spec(scale, nb),
            pl.BlockSpec((D_MODEL, LANES), lambda b, i: (0, 0)),
        ],
        out_specs=[
            pl.BlockSpec((1, tm, D_MODEL), lambda b, i: (b, i, 0)),
            pl.BlockSpec((1, tm, LANES), lambda b, i: (b, i, 0)),
        ],
        out_shape=[
            jax.ShapeDtypeStruct(x.shape, BF16),
            jax.ShapeDtypeStruct((nb, t, LANES), F32),
        ],
        compiler_params=_params("arbitrary", "arbitrary"),
        name="modulate_router",
    )(x, g, shift, scale, w_router_padded)


def _matmul_kernel(*refs, n_a, n_extra, n_out, epilogue, row_scaled):
    a_refs = refs[:n_a]
    w_ref = refs[n_a]
    n_w = 2 if row_scaled else 1
    extra = refs[n_a + n_w:n_a + n_w + n_extra]
    outs = refs[n_a + n_w + n_extra:n_a + n_w + n_extra + n_out]
    wbf_ref = refs[-1]

    @pl.when(pl.program_id(1) == 0)
    def _():
        if row_scaled:
            wbf_ref[...] = (w_ref[...] * refs[n_a + 1][...]).astype(BF16)
        else:
            wbf_ref[...] = w_ref[...].astype(BF16)

    acc = None
    off = 0
    for a_ref in a_refs:
        k = a_ref.shape[-1]
        part = _dot(a_ref[...], wbf_ref[off:off + k, :])
        acc = part if acc is None else acc + part
        off += k
    epilogue(acc, extra, outs)


def fused_matmul(a_list, w, layer, col0, ncols, tm, tn, epilogue, extras, extra_specs, out_dtypes, name,
                 w_row_scale=None):
    m = a_list[0].shape[0]
    k_total = w.shape[1]
    assert sum(a.shape[1] for a in a_list) == k_total
    assert m % tm == 0 and ncols % tn == 0 and col0 % tn == 0
    jb = col0 // tn
    in_specs = [pl.BlockSpec((tm, a.shape[1]), lambda j, i: (i, 0)) for a in a_list]
    in_specs.append(pl.BlockSpec((None, k_total, tn), lambda j, i: (layer, 0, jb + j)))
    weights = [w]
    if w_row_scale is not None:
        in_specs.append(pl.BlockSpec((None, k_total, 1), lambda j, i: (layer, 0, 0)))
        weights.append(w_row_scale)
    in_specs.extend(extra_specs)
    out_specs = [pl.BlockSpec((tm, tn), lambda j, i: (i, j)) for _ in out_dtypes]
    out_shape = [jax.ShapeDtypeStruct((m, ncols), dt) for dt in out_dtypes]
    kern = functools.partial(_matmul_kernel, n_a=len(a_list), n_extra=len(extras), n_out=len(out_dtypes),
                             epilogue=epilogue, row_scaled=w_row_scale is not None)
    return pl.pallas_call(
        kern,
        grid=(ncols // tn, m // tm),
        in_specs=in_specs,
        out_specs=out_specs,
        out_shape=out_shape,
        scratch_shapes=[pltpu.VMEM((k_total, tn), BF16)],
        compiler_params=_params("arbitrary", "arbitrary"),
        name=name,
    )(*a_list, *weights, *extras)


def _epi_cast(acc, extra, outs):
    outs[0][...] = acc.astype(outs[0].dtype)


def _epi_gelu(acc, extra, outs):
    outs[0][...] = (0.5 * acc * (1.0 + lax.erf(acc * (1.0 / math.sqrt(2.0))))).astype(outs[0].dtype)


def _epi_dft(acc, extra, outs):
    ch_ref, cl_ref, sh_ref, sl_ref = extra
    a_ref, b_ref = outs
    zh, zl = _split_bf16(acc)
    for g in range(acc.shape[1] // GROUP_DIM):
        sl = slice(g * GROUP_DIM, (g + 1) * GROUP_DIM)
        a_ref[:, sl] = (_dot(zh[:, sl], ch_ref[...]) + _dot(zl[:, sl], ch_ref[...])
                        + _dot(zh[:, sl], cl_ref[...])).astype(a_ref.dtype)
        b_ref[:, sl] = (_dot(zh[:, sl], sh_ref[...]) + _dot(zl[:, sl], sh_ref[...])
                        + _dot(zh[:, sl], sl_ref[...])).astype(b_ref.dtype)


def _epi_residual(acc, extra, outs):
    h_ref, gate_ref = extra
    outs[0][...] = h_ref[...] + gate_ref[0] * acc


def _epi_residual_pos(acc, extra, outs):
    x_ref, pos_ref, gate_ref = extra
    outs[0][...] = x_ref[...] + pos_ref[...] + gate_ref[0] * acc


def _epi_residual_rownorm(acc, extra, outs):
    h_ref, gate_ref, ss_ref = extra
    inv = lax.rsqrt(jnp.sum(ss_ref[...], axis=1, keepdims=True) * (1.0 / D_INNER) + EPS)
    outs[0][...] = h_ref[...] + gate_ref[0] * (acc * inv)


def _gate_spec(gate, t, tm, tn):
    per_batch = t // tm
    if gate.shape[0] == 1:
        return pl.BlockSpec((1, 1, tn), lambda j, i: (0, 0, j))
    return pl.BlockSpec((1, 1, tn), lambda j, i: (i // per_batch, 0, j))


def _residual_specs(h_flat, gate, t, tm, tn):
    return [pl.BlockSpec((tm, tn), lambda j, i: (i, j)), _gate_spec(gate, t, tm, tn)]


def _gmlp_kernel(u_ref, v_ref, vg_ref, ws_ref, bs_ref, o_ref):
    for c in range(v_ref.shape[1] // CHUNK):
        rows = slice(c * CHUNK, (c + 1) * CHUNK)
        v = v_ref[0, rows, :].astype(F32)
        vn = (v * lax.rsqrt(jnp.mean(v * v, axis=-1, keepdims=True) + EPS) * vg_ref[...]).astype(BF16)
        for g in range(A_GROUPS):
            sl = slice(g * GROUP_DIM, (g + 1) * GROUP_DIM)
            s = _dot(ws_ref[g].astype(BF16), vn[:, sl]) + bs_ref[:, g:g + 1]
            o_ref[0, rows, sl] = (u_ref[0, rows, sl].astype(F32) * s).astype(o_ref.dtype)


def chunk_gmlp(a, v_g, w_s, b_s_t, layer):
    nb, t, _ = a.shape
    rows = min(t, 512)
    return pl.pallas_call(
        _gmlp_kernel,
        grid=(nb, t // rows),
        in_specs=[
            pl.BlockSpec((1, rows, A_WIDTH), lambda b, c: (b, c, 0)),
            pl.BlockSpec((1, rows, A_WIDTH), lambda b, c: (b, c, 1)),
            pl.BlockSpec((None, 1, A_WIDTH), lambda b, c: (layer, 0, 0)),
            pl.BlockSpec((None, A_GROUPS, CHUNK, CHUNK), lambda b, c: (layer, 0, 0, 0)),
            pl.BlockSpec((None, CHUNK, A_GROUPS), lambda b, c: (layer, 0, 0)),
        ],
        out_specs=pl.BlockSpec((1, rows, A_WIDTH), lambda b, c: (b, c, 0)),
        out_shape=jax.ShapeDtypeStruct((nb, t, A_WIDTH), BF16),
        compiler_params=_params("arbitrary", "arbitrary"),
        name="chunk_gmlp",
    )(a, a, v_g.reshape(v_g.shape[0], 1, A_WIDTH), w_s, b_s_t)


def _dft_tables(n):
    k = np.arange(n, dtype=np.int64)
    ang = 2.0 * np.pi * ((k[:, None] * k[None, :]) % n).astype(np.float64) / n
    return np.cos(ang), np.sin(ang)


def _hi_lo(table):
    hi = jnp.asarray(table, dtype=F32).astype(BF16)
    lo = (jnp.asarray(table, dtype=F32) - hi.astype(F32)).astype(BF16)
    return hi, lo


def _token_dft_kernel(ct_ref, st_ref, a_ref, b_ref, o_ref, *, scale):
    y = _dot(ct_ref[...], a_ref[0]) - _dot(st_ref[...], b_ref[0])
    o_ref[0] = (y * scale).astype(o_ref.dtype)


def token_dft(a, b):
    nb, t, w = a.shape
    cos_t, sin_t = _dft_tables(t)
    ct = jnp.asarray(cos_t, dtype=F32).astype(BF16)
    st = jnp.asarray(sin_t, dtype=F32).astype(BF16)
    tm = min(t, 512)
    kern = functools.partial(_token_dft_kernel, scale=1.0 / math.sqrt(t * GROUP_DIM))
    return pl.pallas_call(
        kern,
        grid=(nb, t // tm),
        in_specs=[
            pl.BlockSpec((tm, t), lambda b_, i: (i, 0)),
            pl.BlockSpec((tm, t), lambda b_, i: (i, 0)),
            pl.BlockSpec((1, t, w), lambda b_, i: (b_, 0, 0)),
            pl.BlockSpec((1, t, w), lambda b_, i: (b_, 0, 0)),
        ],
        out_specs=pl.BlockSpec((1, tm, w), lambda b_, i: (b_, i, 0)),
        out_shape=jax.ShapeDtypeStruct((nb, t, w), BF16),
        compiler_params=_params("arbitrary", "arbitrary"),
        name="token_dft",
    )(ct, st, a, b)


def mixer_ab(xn, h, gate, ab_w_in, ab_w_out, gm_v_g, gm_w_s, gm_b_s_t, layer, pos=None):
    nb, t, _ = xn.shape
    m = nb * t
    xf = xn.reshape(m, D_MODEL)
    tm = min(m, 512)
    (a,) = fused_matmul([xf], ab_w_in, layer, 0, 2 * A_WIDTH, tm, 1024, _epi_gelu, [], [], [BF16], "ab_in_gelu")
    c128, s128 = _dft_tables(GROUP_DIM)
    tables = [*_hi_lo(c128), *_hi_lo(s128)]
    table_specs = [pl.BlockSpec((GROUP_DIM, GROUP_DIM), lambda j, i: (0, 0)) for _ in tables]
    za, zb = fused_matmul([xf], ab_w_in, layer, 2 * A_WIDTH, B_WIDTH, tm, 1024, _epi_dft, tables, table_specs,
                          [BF16, BF16], "ab_in_dft")
    ya = chunk_gmlp(a.reshape(nb, t, 2 * A_WIDTH), gm_v_g, gm_w_s, gm_b_s_t, layer)
    yb = token_dft(za.reshape(nb, t, B_WIDTH), zb.reshape(nb, t, B_WIDTH))
    hf = h.reshape(m, D_MODEL)
    tn = 1024
    if pos is None:
        epi, extras, specs = _epi_residual, [hf, gate], _residual_specs(hf, gate, t, tm, tn)
    else:
        per_batch = t // tm
        epi, extras = _epi_residual_pos, [hf, pos, gate]
        specs = [pl.BlockSpec((tm, tn), lambda j, i: (i, j)), pl.BlockSpec((tm, tn), lambda j, i: (i % per_batch, j)),
                 _gate_spec(gate, t, tm, tn)]
    (out,) = fused_matmul([ya.reshape(m, A_WIDTH), yb.reshape(m, B_WIDTH)], ab_w_out, layer, 0, D_MODEL, tm, tn,
                          epi, extras, specs, [F32], "ab_out")
    return out.reshape(nb, t, D_MODEL)


def _conv_kernel(prev_ref, main_ref, next_ref, w_ref, b_ref, o_ref, *, rows):
    r = pl.program_id(1)
    last = pl.num_programs(1) - 1
    prev = jnp.where(r > 0, prev_ref[0].astype(F32), 0.0)
    nxt = jnp.where(r < last, next_ref[0].astype(F32), 0.0)
    full = jnp.concatenate([prev, main_ref[0].astype(F32), nxt], axis=0)
    n = full.shape[0]
    w = w_ref[...]
    acc = None
    for k in range(D_CONV):
        shift = (D_CONV // 2 - k) % n
        term = (full if shift == 0 else pltpu.roll(full, shift, 0)) * w[k:k + 1, :]
        acc = term if acc is None else acc + term
    y = acc[BF16_SUBLANES:BF16_SUBLANES + rows, :] + b_ref[...]
    o_ref[0] = _silu(y).astype(o_ref.dtype)


def dwconv_silu(z, conv_w, conv_b, layer):
    nb, t, ch = z.shape
    rows = min(t, 512)
    tc = 1024
    halo = BF16_SUBLANES
    per = rows // halo
    n_halo = t // halo
    kern = functools.partial(_conv_kernel, rows=rows)
    return pl.pallas_call(
        kern,
        grid=(nb, t // rows, ch // tc),
        in_specs=[
            pl.BlockSpec((1, halo, tc), lambda b, r, c: (b, jnp.maximum(r * per - 1, 0), c)),
            pl.BlockSpec((1, rows, tc), lambda b, r, c: (b, r, c)),
            pl.BlockSpec((1, halo, tc), lambda b, r, c: (b, jnp.minimum((r + 1) * per, n_halo - 1), c)),
            pl.BlockSpec((None, D_CONV, tc), lambda b, r, c: (layer, 0, c)),
            pl.BlockSpec((None, 1, tc), lambda b, r, c: (layer, 0, c)),
        ],
        out_specs=pl.BlockSpec((1, rows, tc), lambda b, r, c: (b, r, c)),
        out_shape=jax.ShapeDtypeStruct(z.shape, BF16),
        compiler_params=_params("arbitrary", "arbitrary", "arbitrary"),
        name="dwconv_silu",
    )(z, z, z, conv_w, conv_b.reshape(conv_b.shape[0], 1, conv_b.shape[1]))


def _softplus(x):
    return jnp.maximum(x, 0.0) + jnp.log(1.0 + jnp.exp(-jnp.abs(x)))


def _prefix_sum(x, axis):
    n = x.shape[axis]
    idx = lax.broadcasted_iota(jnp.int32, x.shape, axis)
    s = 1
    while s < n:
        x = x + jnp.where(idx >= s, pltpu.roll(x, s, axis), 0.0)
        s *= 2
    return x


def _ssd_prep_kernel(raw_ref, bias_ref, a_ref, cumc_ref, cumr_ref, dtr_ref, wr_ref):
    L = SSD_CHUNK
    gw = 2 * HEADS_PER_GROUP
    lane = lax.broadcasted_iota(jnp.int32, (L, raw_ref.shape[2]), 1)
    is_bwd = (lane % gw) >= HEADS_PER_GROUP

    def chunk(c, carry):
        rows = pl.ds(pl.multiple_of(c * L, L), L)
        dt = _softplus(raw_ref[0, rows, :] + bias_ref[...])
        la = dt * (a_ref[...] * LOG2E)
        cum = _prefix_sum(la, 0)
        tot = cum[L - 1:L, :]
        cum = jnp.where(is_bwd, tot - cum + la, cum)
        w = jnp.exp2(tot - cum) * dt
        cum_t = cum.T
        dt_t = dt.T
        w_t = w.T
        for g in range(N_BC_GROUPS):
            ls = slice(g * gw, (g + 1) * gw)
            cumc_ref[0, g, rows, :] = cum[:, ls]
            cumr_ref[0, g, :, rows] = cum_t[ls, :]
            dtr_ref[0, g, :, rows] = dt_t[ls, :]
            wr_ref[0, g, :, rows] = w_t[ls, :]
        return carry

    lax.fori_loop(0, raw_ref.shape[1] // L, chunk, 0)


def ssd_prep(dt_raw, bias, a):
    nb, t, n = dt_raw.shape
    g, gw = N_BC_GROUPS, 2 * HEADS_PER_GROUP
    col = jax.ShapeDtypeStruct((nb, g, t, gw), F32)
    row = jax.ShapeDtypeStruct((nb, g, gw, t), F32)
    col_spec = pl.BlockSpec((1, g, t, gw), lambda b: (b, 0, 0, 0))
    row_spec = pl.BlockSpec((1, g, gw, t), lambda b: (b, 0, 0, 0))
    return pl.pallas_call(
        _ssd_prep_kernel,
        grid=(nb,),
        in_specs=[
            pl.BlockSpec((1, t, n), lambda b: (b, 0, 0)),
            pl.BlockSpec((1, n), lambda b: (0, 0)),
            pl.BlockSpec((1, n), lambda b: (0, 0)),
        ],
        out_specs=[col_spec, row_spec, row_spec, row_spec],
        out_shape=[col, row, row, row],
        compiler_params=_params("arbitrary"),
        name="ssd_prep",
    )(dt_raw, bias, a)


def _ssd_chunk(c, direction, with_y, x_ref, b_ref, c_ref, cumc_ref, cumr_ref, dtr_ref, wr_ref, y_ref, state_ref):
    L = SSD_CHUNK
    hp = HEADS_PER_GROUP
    h0 = direction * hp
    rows = pl.ds(pl.multiple_of(c * L, L), L)
    lane_lo = lax.broadcasted_iota(jnp.int32, (L, 2 * HEADDIM), 1) < HEADDIM
    row_lo = lane_lo[:1]
    edge = L - 1 if direction == 0 else 0

    bm = b_ref[0, rows, :]
    bm_t = bm.astype(F32).T
    xc = x_ref[0, rows, :]
    cumr = cumr_ref[0, 0, :, rows]
    wr = wr_ref[0, 0, :, rows]
    s_prev = state_ref[direction]

    if with_y:
        cm = c_ref[0, rows, :]
        cb = lax.dot_general(cm, bm, (((1,), (1,)), ((), ())), preferred_element_type=F32)
        ii = lax.broadcasted_iota(jnp.int32, (L, L), 0)
        jj = lax.broadcasted_iota(jnp.int32, (L, L), 1)
        causal = (jj <= ii) if direction == 0 else (jj >= ii)
        cbm = jnp.where(causal, cb, 0.0)
        cumc = cumc_ref[0, 0, rows, :]
        dtr = dtr_ref[0, 0, :, rows]
        y_inter = _dot(cm, s_prev.astype(BF16))

    ys, states = [], []
    for p in range(hp // 2):
        k1, k2 = h0 + 2 * p, h0 + 2 * p + 1
        cols = slice(p * 2 * HEADDIM, (p + 1) * 2 * HEADDIM)
        xp = xc[:, cols]
        zero = jnp.zeros_like(xp)
        rhs2 = jnp.concatenate([jnp.where(lane_lo, xp, zero), jnp.where(lane_lo, zero, xp)], axis=0)
        if with_y:
            ms, cum_b = [], []
            for k in (k1, k2):
                cum_b.append(jnp.broadcast_to(cumc[:, k:k + 1], (L, L)))
                seg = jnp.minimum(cum_b[-1] - cumr[k:k + 1, :], 0.0)
                ms.append((cbm * jnp.exp2(seg) * dtr[k:k + 1, :]).astype(BF16))
            edec = jnp.exp2(jnp.where(lane_lo, cum_b[0], cum_b[1]))
            ys.append(_dot(jnp.concatenate(ms, axis=1), rhs2) + y_inter[:, cols] * edec)
        bw = jnp.concatenate([(bm_t * wr[k1:k1 + 1, :]).astype(BF16), (bm_t * wr[k2:k2 + 1, :]).astype(BF16)], axis=1)
        d1 = jnp.exp2(cumr[k1:k1 + 1, edge:edge + 1])
        d2 = jnp.exp2(cumr[k2:k2 + 1, edge:edge + 1])
        dec = jnp.where(row_lo, jnp.broadcast_to(d1, row_lo.shape), jnp.broadcast_to(d2, row_lo.shape))
        states.append(s_prev[:, cols] * dec + _dot(bw, rhs2))
    state_ref[direction] = jnp.concatenate(states, axis=1)
    if with_y:
        y_ref[direction, rows, :] = jnp.concatenate(ys, axis=1)


def _ssd_kernel(*refs, n_chunks, with_y):
    if with_y:
        (x_ref, b_ref, c_ref, z_ref, cumc_ref, cumr_ref, dtr_ref, wr_ref, dskip_ref, s0_ref,
         y_out_ref, ss_ref, sfin_ref, state_ref, y_ref) = refs
    else:
        x_ref, b_ref, cumr_ref, wr_ref, s0_ref, sfin_ref, state_ref = refs
        c_ref = z_ref = cumc_ref = dtr_ref = y_ref = None
    state_ref[...] = s0_ref[:, 0, 0]
    args = (x_ref, b_ref, c_ref, cumc_ref, cumr_ref, dtr_ref, wr_ref, y_ref, state_ref)

    def both(c, carry):
        _ssd_chunk(c, 0, with_y, *args)
        _ssd_chunk(n_chunks - 1 - c, 1, with_y, *args)
        return carry

    lax.fori_loop(0, n_chunks, both, 0)
    sfin_ref[:, 0, 0] = state_ref[...]

    if with_y:
        def gate(c, carry):
            rows = pl.ds(pl.multiple_of(c * SSD_CHUNK, SSD_CHUNK), SSD_CHUNK)
            y = y_ref[0, rows, :] + y_ref[1, rows, :] + x_ref[0, rows, :].astype(F32) * dskip_ref[0]
            yg = (y * _silu(z_ref[0, rows, :].astype(F32))).astype(y_out_ref.dtype)
            y_out_ref[0, rows, :] = yg
            ygf = yg.astype(F32)
            ss_ref[0, 0, rows, :] = jnp.sum(ygf * ygf, axis=1, keepdims=True)
            return carry

        lax.fori_loop(0, n_chunks, gate, 0)


def ssd_scan(xbc, prep, d_skip, s0, z=None):
    nb, t, _ = xbc.shape
    g, gw = N_BC_GROUPS, 2 * HEADS_PER_GROUP
    cumc, cumr, dtr, wr = prep
    with_y = z is not None
    b_off = D_INNER // D_STATE
    c_off = (D_INNER + GN) // D_STATE
    x_spec = pl.BlockSpec((1, t, GROUP_WIDTH), lambda b, gi: (b, 0, gi))
    b_spec = pl.BlockSpec((1, t, D_STATE), lambda b, gi: (b, 0, b_off + gi))
    c_spec = pl.BlockSpec((1, t, D_STATE), lambda b, gi: (b, 0, c_off + gi))
    col_spec = pl.BlockSpec((1, 1, t, gw), lambda b, gi: (b, gi, 0, 0))
    row_spec = pl.BlockSpec((1, 1, gw, t), lambda b, gi: (b, gi, 0, 0))
    s_spec = pl.BlockSpec((2, 1, 1, D_STATE, GROUP_WIDTH), lambda b, gi: (0, b, gi, 0, 0))
    s_shape = jax.ShapeDtypeStruct(s0.shape, F32)
    state = pltpu.VMEM((2, D_STATE, GROUP_WIDTH), F32)
    kern = functools.partial(_ssd_kernel, n_chunks=t // SSD_CHUNK, with_y=with_y)
    if not with_y:
        return pl.pallas_call(
            kern,
            grid=(nb, g),
            in_specs=[x_spec, b_spec, row_spec, row_spec, s_spec],
            out_specs=s_spec,
            out_shape=s_shape,
            scratch_shapes=[state],
            compiler_params=_params("arbitrary", "arbitrary"),
            name="ssd_states",
        )(xbc, xbc, cumr, wr, s0)
    dskip = jnp.repeat(d_skip, HEADDIM).reshape(g, 1, GROUP_WIDTH)
    return pl.pallas_call(
        kern,
        grid=(nb, g),
        in_specs=[x_spec, b_spec, c_spec, x_spec, col_spec, row_spec, row_spec, row_spec,
                  pl.BlockSpec((1, 1, GROUP_WIDTH), lambda b, gi: (gi, 0, 0)), s_spec],
        out_specs=[x_spec, pl.BlockSpec((1, 1, t, 1), lambda b, gi: (b, gi, 0, 0)), s_spec],
        out_shape=[jax.ShapeDtypeStruct((nb, t, D_INNER), BF16), jax.ShapeDtypeStruct((nb, g, t, 1), F32), s_shape],
        scratch_shapes=[state, pltpu.VMEM((2, t, GROUP_WIDTH), F32)],
        compiler_params=_params("arbitrary", "arbitrary"),
        name="ssd_scan",
    )(xbc, xbc, xbc, z, cumc, cumr, dtr, wr, dskip, s0)


def _group_major(v):
    return v.reshape(2, N_BC_GROUPS, HEADS_PER_GROUP).transpose(1, 0, 2).reshape(1, 2 * N_HEADS)


def ssd_project(xn, w_in, w_dt, conv_w, conv_b, dt_bias, a, layer, with_z):
    nb, t, _ = xn.shape
    m = nb * t
    xf = xn.reshape(m, D_MODEL)
    tm = min(m, 512)
    conv_cols = CONV_DIM if with_z else D_INNER + GN
    (xbc,) = fused_matmul([xf], w_in, layer, D_INNER, conv_cols, tm, 1024, _epi_cast, [], [], [BF16], "ssd_in_xbc")
    (dt_raw,) = fused_matmul([xf], w_dt, 0, 0, 2 * N_HEADS, tm, 2 * N_HEADS, _epi_cast, [], [], [F32], "ssd_in_dt")
    xbc = dwconv_silu(xbc.reshape(nb, t, conv_cols), conv_w, conv_b, layer)
    prep = ssd_prep(dt_raw.reshape(nb, t, 2 * N_HEADS), _group_major(dt_bias), _group_major(a))
    z = None
    if with_z:
        (z,) = fused_matmul([xf], w_in, layer, 0, D_INNER, tm, 1024, _epi_cast, [], [], [BF16], "ssd_in_z")
        z = z.reshape(nb, t, D_INNER)
    return xbc, prep, z


def ssd_mix(xn, xc, h, gate, w_in, conv_w, conv_b, dt_bias, a_log, d_skip, norm_g, w_out, layer):
    nb, t, _ = xn.shape
    m = nb * t
    a = -jnp.exp(a_log[layer].astype(F32))
    w_dt = w_in[layer, :, D_INNER + CONV_DIM:].reshape(D_MODEL, 2, N_BC_GROUPS, HEADS_PER_GROUP)
    w_dt = w_dt.transpose(0, 2, 1, 3).reshape(1, D_MODEL, 2 * N_HEADS)
    zero = jnp.zeros((2, nb, N_BC_GROUPS, D_STATE, GROUP_WIDTH), F32)
    xbc_c, prep_c, _ = ssd_project(xc, w_in, w_dt, conv_w, conv_b, dt_bias[layer], a, layer, False)
    s_ctx = ssd_scan(xbc_c, prep_c, d_skip[layer], zero)
    xbc, prep, z = ssd_project(xn, w_in, w_dt, conv_w, conv_b, dt_bias[layer], a, layer, True)
    y, ss, _ = ssd_scan(xbc, prep, d_skip[layer], s_ctx, z)
    ss = ss.reshape(nb, N_BC_GROUPS, t).transpose(0, 2, 1).reshape(m, N_BC_GROUPS)
    hf = h.reshape(m, D_MODEL)
    tm, tn = 512, 512
    specs = _residual_specs(hf, gate, t, tm, tn) + [pl.BlockSpec((tm, N_BC_GROUPS), lambda j, i: (i, 0))]
    (out,) = fused_matmul([y.reshape(m, D_INNER)], w_out, layer, 0, D_MODEL, tm, tn, _epi_residual_rownorm,
                          [hf, gate, ss], specs, [F32], "ssd_out",
                          w_row_scale=norm_g.reshape(norm_g.shape[0], D_INNER, 1))
    return out.reshape(nb, t, D_MODEL)


def _route_kernel(p_ref, tri_ref, pos_ref, post_ref, gate_ref, *, cap):
    probs = p_ref[0]
    t = probs.shape[0]
    pe = probs.T[:N_EXPERTS, :]
    bits = pltpu.bitcast(pe, jnp.int32)

    def step(i, v):
        cand = v | jnp.left_shift(jnp.int32(1), 30 - i)
        cnt = jnp.sum((bits >= cand).astype(F32), axis=1, keepdims=True)
        return jnp.where(cnt >= cap, cand, v)

    thr = lax.fori_loop(0, 31, step, jnp.zeros((N_EXPERTS, 1), jnp.int32))
    gt = bits > thr
    eq = bits == thr
    need = cap - jnp.sum(gt.astype(F32), axis=1, keepdims=True)
    tri = tri_ref[...]
    eq_rank = _dot(eq.astype(BF16), tri)
    sel = gt | (eq & (eq_rank <= need))
    pos = _dot(sel.astype(BF16), tri) - 1.0
    pos = jnp.where(sel, pos, -1.0)
    pos_ref[0] = pos
    pos_full = jnp.concatenate([pos, jnp.full((LANES - N_EXPERTS, t), -1.0, F32)], axis=0)
    pos_t = pos_full.T
    post_ref[0] = pos_t
    gate = jnp.where(pos_t >= 0.0, probs, 0.0)
    hi = gate.astype(BF16).astype(F32)
    mid = (gate - hi).astype(BF16).astype(F32)
    lo = (gate - hi - mid).astype(BF16).astype(F32)
    gate_ref[0] = (hi + pltpu.roll(mid, N_EXPERTS, 1) + pltpu.roll(lo, 2 * N_EXPERTS, 1)).astype(BF16)


def route(probs, cap):
    nb, t, _ = probs.shape
    idx = np.arange(t)
    tri = jnp.asarray((idx[:, None] <= idx[None, :]).astype(np.float32), dtype=BF16)
    kern = functools.partial(_route_kernel, cap=cap)
    return pl.pallas_call(
        kern,
        grid=(nb,),
        in_specs=[
            pl.BlockSpec((1, t, LANES), lambda b: (b, 0, 0)),
            pl.BlockSpec((t, t), lambda b: (0, 0)),
        ],
        out_specs=[
            pl.BlockSpec((1, N_EXPERTS, t), lambda b: (b, 0, 0)),
            pl.BlockSpec((1, t, LANES), lambda b: (b, 0, 0)),
            pl.BlockSpec((1, t, LANES), lambda b: (b, 0, 0)),
        ],
        out_shape=[
            jax.ShapeDtypeStruct((nb, N_EXPERTS, t), F32),
            jax.ShapeDtypeStruct((nb, t, LANES), F32),
            jax.ShapeDtypeStruct((nb, t, LANES), BF16),
        ],
        compiler_params=_params("arbitrary"),
        name="route",
    )(probs, tri)


def _gather_kernel(pos_ref, x_ref, gate_ref, o_ref, gs_ref, *, cap):
    e = pl.program_id(1)
    pos = pos_ref[0, 0]
    t = pos.shape[1]
    slot = lax.broadcasted_iota(jnp.int32, (cap, t), 0).astype(F32)
    onehot = (slot == pos).astype(BF16)
    o_ref[0] = _dot(onehot, x_ref[0]).astype(o_ref.dtype)
    pieces = _dot(onehot, gate_ref[0])
    lane = lax.broadcasted_iota(jnp.int32, pieces.shape, 1)
    mine = (lane < 3 * N_EXPERTS) & (lane % N_EXPERTS == e)
    gs_ref[0] = jnp.sum(jnp.where(mine, pieces, 0.0), axis=1, keepdims=True)


def gather_tokens(pos, xm, gate_split, cap):
    nb, t, _ = xm.shape
    kern = functools.partial(_gather_kernel, cap=cap)
    return pl.pallas_call(
        kern,
        grid=(nb, N_EXPERTS),
        in_specs=[
            pl.BlockSpec((1, 1, 1, t), lambda b, e: (b, e, 0, 0)),
            pl.BlockSpec((1, t, D_MODEL), lambda b, e: (b, 0, 0)),
            pl.BlockSpec((1, t, LANES), lambda b, e: (b, 0, 0)),
        ],
        out_specs=[
            pl.BlockSpec((1, cap, D_MODEL), lambda b, e: (e, b, 0)),
            pl.BlockSpec((1, cap, 1), lambda b, e: (e, b, 0)),
        ],
        out_shape=[
            jax.ShapeDtypeStruct((N_EXPERTS, nb * cap, D_MODEL), BF16),
            jax.ShapeDtypeStruct((N_EXPERTS, nb * cap, 1), F32),
        ],
        compiler_params=_params("arbitrary", "arbitrary"),
        name="gather_tokens",
    )(pos.reshape(nb, N_EXPERTS, 1, t), xm, gate_split)


def _ffn_kernel(x_ref, gs_ref, wg_ref, wu_ref, wd_ref, o_ref, acc_ref):
    f = pl.program_id(2)
    x = x_ref[0]
    gate = _dot(x, wg_ref[...].astype(BF16))
    up = _dot(x, wu_ref[...].astype(BF16))
    hid = (_silu(gate) * up).astype(BF16)

    @pl.when(f == 0)
    def _():
        acc_ref[...] = jnp.zeros_like(acc_ref)

    acc_ref[...] += _dot(hid, wd_ref[...].astype(BF16))

    @pl.when(f == pl.num_programs(2) - 1)
    def _():
        o_ref[0] = (acc_ref[...] * gs_ref[0]).astype(o_ref.dtype)


def expert_ffn(xe, gate_slot, w_gate, w_up, w_down, layer):
    _, m, _ = xe.shape
    tm = min(m, 1024)
    tf = 256
    return pl.pallas_call(
        _ffn_kernel,
        grid=(N_EXPERTS, m // tm, D_MODEL // tf),
        in_specs=[
            pl.BlockSpec((1, tm, D_MODEL), lambda e, i, f: (e, i, 0)),
            pl.BlockSpec((1, tm, 1), lambda e, i, f: (e, i, 0)),
            pl.BlockSpec((None, None, D_MODEL, tf), lambda e, i, f: (layer, e, 0, f)),
            pl.BlockSpec((None, None, D_MODEL, tf), lambda e, i, f: (layer, e, 0, f)),
            pl.BlockSpec((None, None, tf, D_MODEL), lambda e, i, f: (layer, e, f, 0)),
        ],
        out_specs=pl.BlockSpec((1, tm, D_MODEL), lambda e, i, f: (e, i, 0)),
        out_shape=jax.ShapeDtypeStruct(xe.shape, BF16),
        scratch_shapes=[pltpu.VMEM((tm, D_MODEL), F32)],
        compiler_params=_params("arbitrary", "arbitrary", "arbitrary"),
        name="expert_ffn",
    )(xe, gate_slot, w_gate, w_up, w_down)


def _combine_kernel(*refs, cap, final_norm):
    post_ref, y_ref, h_ref, g2_ref = refs[:4]
    o_ref = refs[-1]
    tq = post_ref.shape[1]
    slot = lax.broadcasted_iota(jnp.int32, (tq, cap), 1).astype(F32)
    pos_t = post_ref[0]
    acc = None
    for e in range(N_EXPERTS):
        onehot = (slot == pos_t[:, e:e + 1]).astype(BF16)
        part = _dot(onehot, y_ref[e])
        acc = part if acc is None else acc + part
    out = h_ref[0] + g2_ref[0] * acc
    if final_norm:
        out = out * lax.rsqrt(jnp.mean(out * out, axis=-1, keepdims=True) + EPS) * refs[4][...]
    o_ref[0] = out


def combine(pos_t, ye, h, g2, cap, final_g=None):
    nb, t, _ = h.shape
    tq = min(t, 256)
    if g2.shape[0] == 1:
        g2_spec = pl.BlockSpec((1, 1, D_MODEL), lambda b, i: (0, 0, 0))
    else:
        g2_spec = pl.BlockSpec((1, 1, D_MODEL), lambda b, i: (b, 0, 0))
    in_specs = [
        pl.BlockSpec((1, tq, LANES), lambda b, i: (b, i, 0)),
        pl.BlockSpec((N_EXPERTS, cap, D_MODEL), lambda b, i: (0, b, 0)),
        pl.BlockSpec((1, tq, D_MODEL), lambda b, i: (b, i, 0)),
        g2_spec,
    ]
    args = [pos_t, ye, h, g2]
    if final_g is not None:
        in_specs.append(pl.BlockSpec((1, D_MODEL), lambda b, i: (0, 0)))
        args.append(final_g)
    kern = functools.partial(_combine_kernel, cap=cap, final_norm=final_g is not None)
    return pl.pallas_call(
        kern,
        grid=(nb, t // tq),
        in_specs=in_specs,
        out_specs=pl.BlockSpec((1, tq, D_MODEL), lambda b, i: (b, i, 0)),
        out_shape=jax.ShapeDtypeStruct(h.shape, F32),
        compiler_params=_params("arbitrary", "arbitrary"),
        name="combine",
    )(*args)


def expert_choice_ffn(h, g, shift, scale, g2, w_router, w_gate, w_up, w_down, layer, final_g=None):
    nb, t, _ = h.shape
    cap = CAPACITY_FACTOR * t // N_EXPERTS
    wr = jnp.pad(w_router[layer], ((0, 0), (0, LANES - N_EXPERTS)))
    xm, probs = modulate_router(h, g, shift, scale, wr)
    pos, pos_t, gate_split = route(probs, cap)
    xe, gate_slot = gather_tokens(pos, xm, gate_split, cap)
    ye = expert_ffn(xe, gate_slot, w_gate, w_up, w_down, layer)
    return combine(pos_t, ye, h, g2, cap, final_g)


def _sincos_2d(rows, cols, dim):
    quarter = dim // 4
    omega = 1.0 / (10000.0 ** (jnp.arange(quarter, dtype=F32) / quarter))
    r = jnp.repeat(jnp.arange(rows, dtype=F32), cols)[:, None] * omega
    cl = jnp.tile(jnp.arange(cols, dtype=F32), rows)[:, None] * omega
    return jnp.concatenate([jnp.sin(r), jnp.cos(r), jnp.sin(cl), jnp.cos(cl)], axis=-1)


def kernel(x, c, ctx, c_ctx, mod_w, mod_b, norm_g, final_g, ab_w_in, ab_w_out, gm_v_g, gm_w_s, gm_b_s,
           ssd_w_in, ssd_conv_w, ssd_conv_b, ssd_dt_bias, ssd_a_log, ssd_d, ssd_norm_g, ssd_w_out,
           moe_w_router, moe_w_gate, moe_w_up, moe_w_down):
    nb, n, _ = x.shape
    depth = mod_w.shape[0]
    pos = _sincos_2d(n // GRID_W, GRID_W, D_MODEL)
    h = x
    hc = ctx
    cond = jnp.concatenate([c, c_ctx[None], jnp.zeros((16 - nb - 1, D_MODEL), F32)], axis=0)
    gm_b_s_t = jnp.swapaxes(gm_b_s, 1, 2)
    for i in range(depth):
        last = i == depth - 1
        even = i % 2 == 0
        j = i // 2
        ctx_reaches_latent = (not last) or (not even)
        mod = adaln(cond, mod_w, mod_b, i).reshape(16, 6, 1, D_MODEL)
        sh1, sc1, g1, sh2, sc2, g2 = (mod[:nb, k] for k in range(6))
        csh1, csc1, cg1, csh2, csc2, cg2 = (mod[nb:nb + 1, k] for k in range(6))
        g_mix = norm_g[i, 0][None]
        g_ffn = norm_g[i, 1][None]
        if pos is not None and not even:
            h, pos = h + pos[None], None
        xn = modulate(h, g_mix, sh1, sc1, pos)
        if ctx_reaches_latent:
            xc = modulate(hc, g_mix, csh1, csc1)
        if even:
            h = mixer_ab(xn, h, g1, ab_w_in, ab_w_out, gm_v_g, gm_w_s, gm_b_s_t, j, pos)
            pos = None
            if not last:
                hc = mixer_ab(xc, hc, cg1, ab_w_in, ab_w_out, gm_v_g, gm_w_s, gm_b_s_t, j)
        else:
            assert last, "context outputs of a state-space layer are only needed before the last layer"
            h = ssd_mix(xn, xc, h, g1, ssd_w_in, ssd_conv_w, ssd_conv_b, ssd_dt_bias, ssd_a_log, ssd_d,
                        ssd_norm_g, ssd_w_out, j)
        h = expert_choice_ffn(h, g_ffn, sh2, sc2, g2, moe_w_router, moe_w_gate, moe_w_up, moe_w_down, i,
                              final_g[None] if last else None)
        if not last:
            hc = expert_choice_ffn(hc, g_ffn, csh2, csc2, cg2, moe_w_router, moe_w_gate, moe_w_up, moe_w_down, i)
    return h
```

```python
import functools
import math

import jax
import jax.numpy as jnp
import numpy as np
from jax import lax
from jax.experimental import pallas as pl
from jax.experimental.pallas import tpu as pltpu

F32 = jnp.float32
BF16 = jnp.bfloat16

D_MODEL = 2048
GRID_W = 64
EPS = 1e-6

A_WIDTH = D_MODEL // 2
GROUP_DIM = 128
A_GROUPS = A_WIDTH // GROUP_DIM
CHUNK = 128
B_WIDTH = D_MODEL // 2

D_INNER = 2 * D_MODEL
HEADDIM = 64
N_HEADS = D_INNER // HEADDIM
D_STATE = 128
N_BC_GROUPS = 8
HEADS_PER_GROUP = N_HEADS // N_BC_GROUPS
D_CONV = 5
SSD_CHUNK = 128
GN = N_BC_GROUPS * D_STATE
CONV_DIM = D_INNER + 2 * GN
GROUP_WIDTH = HEADS_PER_GROUP * HEADDIM

N_EXPERTS = 16
CAPACITY_FACTOR = 2

LANES = 128
BF16_SUBLANES = 16
VMEM_LIMIT = 56 * 1024 * 1024
LOG2E = 1.4426950408889634


def _params(*sem):
    return pltpu.CompilerParams(dimension_semantics=sem, vmem_limit_bytes=VMEM_LIMIT)


def _silu(x):
    return x / (1.0 + jnp.exp(-x))


def _split_bf16(x):
    hi = x.astype(BF16)
    lo = (x - hi.astype(F32)).astype(BF16)
    return hi, lo


def _dot(a, b):
    return jnp.dot(a, b, preferred_element_type=F32)


def _adaln_kernel(c_ref, w_ref, b_ref, o_ref):
    a = _silu(c_ref[...]).astype(BF16)
    o_ref[...] = _dot(a, w_ref[...].astype(BF16)) + b_ref[...]


def adaln(cond, mod_w, mod_b, layer):
    rows = cond.shape[0]
    tn = 1024
    n = 6 * D_MODEL
    return pl.pallas_call(
        _adaln_kernel,
        grid=(n // tn,),
        in_specs=[
            pl.BlockSpec((rows, D_MODEL), lambda j: (0, 0)),
            pl.BlockSpec((None, D_MODEL, tn), lambda j: (layer, 0, j)),
            pl.BlockSpec((None, 1, tn), lambda j: (layer, 0, j)),
        ],
        out_specs=pl.BlockSpec((rows, tn), lambda j: (0, j)),
        out_shape=jax.ShapeDtypeStruct((rows, n), F32),
        compiler_params=_params("arbitrary"),
        name="adaln",
    )(cond, mod_w, mod_b.reshape(mod_b.shape[0], 1, n))


def _rms_modulate(x, g, shift, scale):
    ms = jnp.mean(x * x, axis=-1, keepdims=True)
    y = x * lax.rsqrt(ms + EPS) * g
    return y * (1.0 + scale) + shift


def _modulate_kernel(*refs, with_pos):
    if with_pos:
        x_ref, pos_ref, g_ref, sh_ref, sc_ref, o_ref = refs
        x = x_ref[0] + pos_ref[...]
    else:
        x_ref, g_ref, sh_ref, sc_ref, o_ref = refs
        x = x_ref[0]
    o_ref[0] = _rms_modulate(x, g_ref[...], sh_ref[0], sc_ref[0]).astype(o_ref.dtype)


def _row_vec_spec(vec, n_batch):
    if vec.shape[0] == 1:
        return pl.BlockSpec((1, 1, D_MODEL), lambda b, i: (0, 0, 0))
    assert vec.shape[0] == n_batch
    return pl.BlockSpec((1, 1, D_MODEL), lambda b, i: (b, 0, 0))


def modulate(x, g, shift, scale, pos=None):
    nb, t, _ = x.shape
    tm = min(t, 512)
    in_specs = [pl.BlockSpec((1, tm, D_MODEL), lambda b, i: (b, i, 0))]
    args = [x]
    if pos is not None:
        in_specs.append(pl.BlockSpec((tm, D_MODEL), lambda b, i: (i, 0)))
        args.append(pos)
    in_specs += [pl.BlockSpec((1, D_MODEL), lambda b, i: (0, 0)), _row_vec_spec(shift, nb), _row_vec_spec(scale, nb)]
    return pl.pallas_call(
        functools.partial(_modulate_kernel, with_pos=pos is not None),
        grid=(nb, t // tm),
        in_specs=in_specs,
        out_specs=pl.BlockSpec((1, tm, D_MODEL), lambda b, i: (b, i, 0)),
        out_shape=jax.ShapeDtypeStruct(x.shape, BF16),
        compiler_params=_params("arbitrary", "arbitrary"),
        name="modulate",
    )(*args, g, shift, scale)


def _modulate_router_kernel(x_ref, g_ref, sh_ref, sc_ref, wr_ref, o_ref, p_ref):
    xm = _rms_modulate(x_ref[0], g_ref[...], sh_ref[0], sc_ref[0])
    xh, xl = _split_bf16(xm)
    wh, wl = _split_bf16(wr_ref[...])
    logits = _dot(xh, wh) + _dot(xl, wh) + _dot(xh, wl)
    lane = lax.broadcasted_iota(jnp.int32, logits.shape, 1)
    logits = jnp.where(lane < N_EXPERTS, logits, -jnp.inf)
    e = jnp.exp(logits - jnp.max(logits, axis=-1, keepdims=True))
    p_ref[0] = e / jnp.sum(e, axis=-1, keepdims=True)
    o_ref[0] = xh


def modulate_router(x, g, shift, scale, w_router_padded):
    nb, t, _ = x.shape
    tm = min(t, 512)
    return pl.pallas_call(
        _modulate_router_kernel,
        grid=(nb, t // tm),
        in_specs=[
            pl.BlockSpec((1, tm, D_MODEL), lambda b, i: (b, i, 0)),
            pl.BlockSpec((1, D_MODEL), lambda b, i: (0, 0)),
            _row_vec_spec(shift, nb),
            _row_vec_spec(scale, nb),
            pl.BlockSpec((D_MODEL, LANES), lambda b, i: (0, 0)),
        ],
        out_specs=[
            pl.BlockSpec((1, tm, D_MODEL), lambda b, i: (b, i, 0)),
            pl.BlockSpec((1, tm, LANES), lambda b, i: (b, i, 0)),
        ],
        out_shape=[
            jax.ShapeDtypeStruct(x.shape, BF16),
            jax.ShapeDtypeStruct((nb, t, LANES), F32),
        ],
        compiler_params=_params("arbitrary", "arbitrary"),
        name="modulate_router",
    )(x, g, shift, scale, w_router_padded)


def _matmul_kernel(*refs, n_a, n_extra, n_out, epilogue, row_scaled):
    a_refs = refs[:n_a]
    w_ref = refs[n_a]
    n_w = 2 if row_scaled else 1
    extra = refs[n_a + n_w:n_a + n_w + n_extra]
    outs = refs[n_a + n_w + n_extra:n_a + n_w + n_extra + n_out]
    wbf_ref = refs[-1]

    @pl.when(pl.program_id(1) == 0)
    def _():
        if row_scaled:
            wbf_ref[...] = (w_ref[...] * refs[n_a + 1][...]).astype(BF16)
        else:
            wbf_ref[...] = w_ref[...].astype(BF16)

    acc = None
    off = 0
    for a_ref in a_refs:
        k = a_ref.shape[-1]
        part = _dot(a_ref[...], wbf_ref[off:off + k, :])
        acc = part if acc is None else acc + part
        off += k
    epilogue(acc, extra, outs, a_refs)


def fused_matmul(a_list, w, layer, col0, ncols, tm, tn, epilogue, extras, extra_specs, out_dtypes, name,
                 w_row_scale=None):
    m = a_list[0].shape[0]
    k_total = w.shape[1]
    assert sum(a.shape[1] for a in a_list) == k_total
    assert m % tm == 0 and ncols % tn == 0 and col0 % tn == 0
    jb = col0 // tn
    in_specs = [pl.BlockSpec((tm, a.shape[1]), lambda j, i: (i, 0)) for a in a_list]
    in_specs.append(pl.BlockSpec((None, k_total, tn), lambda j, i: (layer, 0, jb + j)))
    weights = [w]
    if w_row_scale is not None:
        in_specs.append(pl.BlockSpec((None, k_total, 1), lambda j, i: (layer, 0, 0)))
        weights.append(w_row_scale)
    in_specs.extend(extra_specs)
    out_specs = [pl.BlockSpec((tm, tn), lambda j, i: (i, j)) for _ in out_dtypes]
    out_shape = [jax.ShapeDtypeStruct((m, ncols), dt) for dt in out_dtypes]
    kern = functools.partial(_matmul_kernel, n_a=len(a_list), n_extra=len(extras), n_out=len(out_dtypes),
                             epilogue=epilogue, row_scaled=w_row_scale is not None)
    return pl.pallas_call(
        kern,
        grid=(ncols // tn, m // tm),
        in_specs=in_specs,
        out_specs=out_specs,
        out_shape=out_shape,
        scratch_shapes=[pltpu.VMEM((k_total, tn), BF16)],
        compiler_params=_params("arbitrary", "arbitrary"),
        name=name,
    )(*a_list, *weights, *extras)


def _epi_cast(acc, extra, outs, a_refs):
    outs[0][...] = acc.astype(outs[0].dtype)


def _epi_gelu(acc, extra, outs, a_refs):
    outs[0][...] = (0.5 * acc * (1.0 + lax.erf(acc * (1.0 / math.sqrt(2.0))))).astype(outs[0].dtype)


def _epi_dft(acc, extra, outs, a_refs):
    ch_ref, cl_ref, sh_ref, sl_ref = extra
    a_ref, b_ref = outs
    zh, zl = _split_bf16(acc)
    for g in range(acc.shape[1] // GROUP_DIM):
        sl = slice(g * GROUP_DIM, (g + 1) * GROUP_DIM)
        a_ref[:, sl] = (_dot(zh[:, sl], ch_ref[...]) + _dot(zl[:, sl], ch_ref[...])
                        + _dot(zh[:, sl], cl_ref[...])).astype(a_ref.dtype)
        b_ref[:, sl] = (_dot(zh[:, sl], sh_ref[...]) + _dot(zl[:, sl], sh_ref[...])
                        + _dot(zh[:, sl], sl_ref[...])).astype(b_ref.dtype)


def _epi_residual(acc, extra, outs, a_refs):
    h_ref, gate_ref = extra
    outs[0][...] = h_ref[...] + gate_ref[0] * acc


def _epi_residual_pos(acc, extra, outs, a_refs):
    x_ref, pos_ref, gate_ref = extra
    outs[0][...] = x_ref[...] + pos_ref[...] + gate_ref[0] * acc


def _epi_residual_rownorm(acc, extra, outs, a_refs):
    h_ref, gate_ref = extra
    a = a_refs[0][...].astype(F32)
    inv = lax.rsqrt(jnp.mean(a * a, axis=1, keepdims=True) + EPS)
    outs[0][...] = h_ref[...] + gate_ref[0] * (acc * inv)


def _gate_spec(gate, t, tm, tn):
    per_batch = t // tm
    if gate.shape[0] == 1:
        return pl.BlockSpec((1, 1, tn), lambda j, i: (0, 0, j))
    return pl.BlockSpec((1, 1, tn), lambda j, i: (i // per_batch, 0, j))


def _residual_specs(h_flat, gate, t, tm, tn):
    return [pl.BlockSpec((tm, tn), lambda j, i: (i, j)), _gate_spec(gate, t, tm, tn)]


def _gmlp_kernel(u_ref, v_ref, vg_ref, ws_ref, bs_ref, o_ref):
    for c in range(v_ref.shape[1] // CHUNK):
        rows = slice(c * CHUNK, (c + 1) * CHUNK)
        v = v_ref[0, rows, :].astype(F32)
        vn = (v * lax.rsqrt(jnp.mean(v * v, axis=-1, keepdims=True) + EPS) * vg_ref[...]).astype(BF16)
        for g in range(A_GROUPS):
            sl = slice(g * GROUP_DIM, (g + 1) * GROUP_DIM)
            s = _dot(ws_ref[g].astype(BF16), vn[:, sl]) + bs_ref[:, g:g + 1]
            o_ref[0, rows, sl] = (u_ref[0, rows, sl].astype(F32) * s).astype(o_ref.dtype)


def chunk_gmlp(a, v_g, w_s, b_s_t, layer):
    nb, t, _ = a.shape
    rows = min(t, 512)
    return pl.pallas_call(
        _gmlp_kernel,
        grid=(nb, t // rows),
        in_specs=[
            pl.BlockSpec((1, rows, A_WIDTH), lambda b, c: (b, c, 0)),
            pl.BlockSpec((1, rows, A_WIDTH), lambda b, c: (b, c, 1)),
            pl.BlockSpec((None, 1, A_WIDTH), lambda b, c: (layer, 0, 0)),
            pl.BlockSpec((None, A_GROUPS, CHUNK, CHUNK), lambda b, c: (layer, 0, 0, 0)),
            pl.BlockSpec((None, CHUNK, A_GROUPS), lambda b, c: (layer, 0, 0)),
        ],
        out_specs=pl.BlockSpec((1, rows, A_WIDTH), lambda b, c: (b, c, 0)),
        out_shape=jax.ShapeDtypeStruct((nb, t, A_WIDTH), BF16),
        compiler_params=_params("arbitrary", "arbitrary"),
        name="chunk_gmlp",
    )(a, a, v_g.reshape(v_g.shape[0], 1, A_WIDTH), w_s, b_s_t)


def _dft_tables(n):
    k = np.arange(n, dtype=np.int64)
    ang = 2.0 * np.pi * ((k[:, None] * k[None, :]) % n).astype(np.float64) / n
    return np.cos(ang), np.sin(ang)


def _hi_lo(table):
    hi = jnp.asarray(table, dtype=F32).astype(BF16)
    lo = (jnp.asarray(table, dtype=F32) - hi.astype(F32)).astype(BF16)
    return hi, lo


def _token_dft_kernel(ct_ref, st_ref, a_ref, b_ref, o_ref, *, scale):
    y = _dot(ct_ref[...], a_ref[0]) - _dot(st_ref[...], b_ref[0])
    o_ref[0] = (y * scale).astype(o_ref.dtype)


def token_dft(a, b):
    nb, t, w = a.shape
    cos_t, sin_t = _dft_tables(t)
    ct = jnp.asarray(cos_t, dtype=F32).astype(BF16)
    st = jnp.asarray(sin_t, dtype=F32).astype(BF16)
    tm = min(t, 512)
    kern = functools.partial(_token_dft_kernel, scale=1.0 / math.sqrt(t * GROUP_DIM))
    return pl.pallas_call(
        kern,
        grid=(nb, t // tm),
        in_specs=[
            pl.BlockSpec((tm, t), lambda b_, i: (i, 0)),
            pl.BlockSpec((tm, t), lambda b_, i: (i, 0)),
            pl.BlockSpec((1, t, w), lambda b_, i: (b_, 0, 0)),
            pl.BlockSpec((1, t, w), lambda b_, i: (b_, 0, 0)),
        ],
        out_specs=pl.BlockSpec((1, tm, w), lambda b_, i: (b_, i, 0)),
        out_shape=jax.ShapeDtypeStruct((nb, t, w), BF16),
        compiler_params=_params("arbitrary", "arbitrary"),
        name="token_dft",
    )(ct, st, a, b)


def mixer_ab(xn, h, gate, ab_w_in, ab_w_out, gm_v_g, gm_w_s, gm_b_s_t, layer, pos=None):
    nb, t, _ = xn.shape
    m = nb * t
    xf = xn.reshape(m, D_MODEL)
    tm = min(m, 512)
    (a,) = fused_matmul([xf], ab_w_in, layer, 0, 2 * A_WIDTH, tm, 1024, _epi_gelu, [], [], [BF16], "ab_in_gelu")
    c128, s128 = _dft_tables(GROUP_DIM)
    tables = [*_hi_lo(c128), *_hi_lo(s128)]
    table_specs = [pl.BlockSpec((GROUP_DIM, GROUP_DIM), lambda j, i: (0, 0)) for _ in tables]
    za, zb = fused_matmul([xf], ab_w_in, layer, 2 * A_WIDTH, B_WIDTH, tm, 1024, _epi_dft, tables, table_specs,
                          [BF16, BF16], "ab_in_dft")
    ya = chunk_gmlp(a.reshape(nb, t, 2 * A_WIDTH), gm_v_g, gm_w_s, gm_b_s_t, layer)
    yb = token_dft(za.reshape(nb, t, B_WIDTH), zb.reshape(nb, t, B_WIDTH))
    hf = h.reshape(m, D_MODEL)
    tn = 1024
    if pos is None:
        epi, extras, specs = _epi_residual, [hf, gate], _residual_specs(hf, gate, t, tm, tn)
    else:
        per_batch = t // tm
        epi, extras = _epi_residual_pos, [hf, pos, gate]
        specs = [pl.BlockSpec((tm, tn), lambda j, i: (i, j)), pl.BlockSpec((tm, tn), lambda j, i: (i % per_batch, j)),
                 _gate_spec(gate, t, tm, tn)]
    (out,) = fused_matmul([ya.reshape(m, A_WIDTH), yb.reshape(m, B_WIDTH)], ab_w_out, layer, 0, D_MODEL, tm, tn,
                          epi, extras, specs, [F32], "ab_out")
    return out.reshape(nb, t, D_MODEL)


def _conv_kernel(prev_ref, main_ref, next_ref, w_ref, b_ref, o_ref, *, rows):
    r = pl.program_id(1)
    last = pl.num_programs(1) - 1
    prev = jnp.where(r > 0, prev_ref[0].astype(F32), 0.0)
    nxt = jnp.where(r < last, next_ref[0].astype(F32), 0.0)
    full = jnp.concatenate([prev, main_ref[0].astype(F32), nxt], axis=0)
    n = full.shape[0]
    w = w_ref[...]
    acc = None
    for k in range(D_CONV):
        shift = (D_CONV // 2 - k) % n
        term = (full if shift == 0 else pltpu.roll(full, shift, 0)) * w[k:k + 1, :]
        acc = term if acc is None else acc + term
    y = acc[BF16_SUBLANES:BF16_SUBLANES + rows, :] + b_ref[...]
    o_ref[0] = _silu(y).astype(o_ref.dtype)


def dwconv_silu(z, conv_w, conv_b, layer):
    nb, t, ch = z.shape
    rows = min(t, 512)
    tc = 1024
    halo = BF16_SUBLANES
    per = rows // halo
    n_halo = t // halo
    kern = functools.partial(_conv_kernel, rows=rows)
    return pl.pallas_call(
        kern,
        grid=(nb, t // rows, ch // tc),
        in_specs=[
            pl.BlockSpec((1, halo, tc), lambda b, r, c: (b, jnp.maximum(r * per - 1, 0), c)),
            pl.BlockSpec((1, rows, tc), lambda b, r, c: (b, r, c)),
            pl.BlockSpec((1, halo, tc), lambda b, r, c: (b, jnp.minimum((r + 1) * per, n_halo - 1), c)),
            pl.BlockSpec((None, D_CONV, tc), lambda b, r, c: (layer, 0, c)),
            pl.BlockSpec((None, 1, tc), lambda b, r, c: (layer, 0, c)),
        ],
        out_specs=pl.BlockSpec((1, rows, tc), lambda b, r, c: (b, r, c)),
        out_shape=jax.ShapeDtypeStruct(z.shape, BF16),
        compiler_params=_params("arbitrary", "arbitrary", "arbitrary"),
        name="dwconv_silu",
    )(z, z, z, conv_w, conv_b.reshape(conv_b.shape[0], 1, conv_b.shape[1]))


def _softplus(x):
    return jnp.maximum(x, 0.0) + jnp.log(1.0 + jnp.exp(-jnp.abs(x)))


def _prefix_sum(x, axis):
    n = x.shape[axis]
    idx = lax.broadcasted_iota(jnp.int32, x.shape, axis)
    s = 1
    while s < n:
        x = x + jnp.where(idx >= s, pltpu.roll(x, s, axis), 0.0)
        s *= 2
    return x


def _ssd_prep_kernel(raw_ref, bias_ref, a_ref, cumc_ref, cumr_ref, dtr_ref, wr_ref):
    L = SSD_CHUNK
    gw = 2 * HEADS_PER_GROUP
    lane = lax.broadcasted_iota(jnp.int32, (L, raw_ref.shape[2]), 1)
    is_bwd = (lane % gw) >= HEADS_PER_GROUP

    def chunk(c, carry):
        rows = pl.ds(pl.multiple_of(c * L, L), L)
        dt = _softplus(raw_ref[0, rows, :] + bias_ref[...])
        la = dt * (a_ref[...] * LOG2E)
        cum = _prefix_sum(la, 0)
        tot = cum[L - 1:L, :]
        cum = jnp.where(is_bwd, tot - cum + la, cum)
        w = jnp.exp2(tot - cum) * dt
        cum_t = cum.T
        dt_t = dt.T
        w_t = w.T
        for g in range(N_BC_GROUPS):
            ls = slice(g * gw, (g + 1) * gw)
            cumc_ref[0, g, rows, :] = cum[:, ls]
            cumr_ref[0, g, :, rows] = cum_t[ls, :]
            dtr_ref[0, g, :, rows] = dt_t[ls, :]
            wr_ref[0, g, :, rows] = w_t[ls, :]
        return carry

    lax.fori_loop(0, raw_ref.shape[1] // L, chunk, 0)


def ssd_prep(dt_raw, bias, a):
    nb, t, n = dt_raw.shape
    g, gw = N_BC_GROUPS, 2 * HEADS_PER_GROUP
    col = jax.ShapeDtypeStruct((nb, g, t, gw), F32)
    row = jax.ShapeDtypeStruct((nb, g, gw, t), F32)
    col_spec = pl.BlockSpec((1, g, t, gw), lambda b: (b, 0, 0, 0))
    row_spec = pl.BlockSpec((1, g, gw, t), lambda b: (b, 0, 0, 0))
    return pl.pallas_call(
        _ssd_prep_kernel,
        grid=(nb,),
        in_specs=[
            pl.BlockSpec((1, t, n), lambda b: (b, 0, 0)),
            pl.BlockSpec((1, n), lambda b: (0, 0)),
            pl.BlockSpec((1, n), lambda b: (0, 0)),
        ],
        out_specs=[col_spec, row_spec, row_spec, row_spec],
        out_shape=[col, row, row, row],
        compiler_params=_params("arbitrary"),
        name="ssd_prep",
    )(dt_raw, bias, a)


def _ssd_chunk(c, direction, with_y, x_ref, b_ref, c_ref, cumc_ref, cumr_ref, dtr_ref, wr_ref, y_ref, state_ref):
    L = SSD_CHUNK
    hp = HEADS_PER_GROUP
    h0 = direction * hp
    rows = pl.ds(pl.multiple_of(c * L, L), L)
    lane_lo = lax.broadcasted_iota(jnp.int32, (L, 2 * HEADDIM), 1) < HEADDIM
    row_lo = lane_lo[:1]
    edge = L - 1 if direction == 0 else 0

    bm = b_ref[0, rows, :]
    bm_t = bm.astype(F32).T
    cumr = cumr_ref[0, 0, :, rows]
    wr = wr_ref[0, 0, :, rows]

    if with_y:
        cm = c_ref[0, rows, :]
        cb = lax.dot_general(cm, bm, (((1,), (1,)), ((), ())), preferred_element_type=F32)
        ii = lax.broadcasted_iota(jnp.int32, (L, L), 0)
        jj = lax.broadcasted_iota(jnp.int32, (L, L), 1)
        causal = (jj <= ii) if direction == 0 else (jj >= ii)
        cbm = jnp.where(causal, cb, 0.0)
        cumc = cumc_ref[0, 0, rows, :]
        dtr = dtr_ref[0, 0, :, rows]

    for p in range(hp // 2):
        k1, k2 = h0 + 2 * p, h0 + 2 * p + 1
        cols = slice(p * 2 * HEADDIM, (p + 1) * 2 * HEADDIM)
        s_prev = state_ref[direction, :, cols]
        xp = x_ref[0, rows, cols]
        zero = jnp.zeros_like(xp)
        rhs2 = jnp.concatenate([jnp.where(lane_lo, xp, zero), jnp.where(lane_lo, zero, xp)], axis=0)
        if with_y:
            ms, cum_b = [], []
            for k in (k1, k2):
                cum_b.append(jnp.broadcast_to(cumc[:, k:k + 1], (L, L)))
                seg = jnp.minimum(cum_b[-1] - cumr[k:k + 1, :], 0.0)
                ms.append((cbm * jnp.exp2(seg) * dtr[k:k + 1, :]).astype(BF16))
            edec = jnp.exp2(jnp.where(lane_lo, cum_b[0], cum_b[1]))
            y_ref[direction, rows, cols] = (_dot(jnp.concatenate(ms, axis=1), rhs2)
                                            + _dot(cm, s_prev.astype(BF16)) * edec)
        bw = jnp.concatenate([(bm_t * wr[k1:k1 + 1, :]).astype(BF16), (bm_t * wr[k2:k2 + 1, :]).astype(BF16)], axis=1)
        d1 = jnp.exp2(cumr[k1:k1 + 1, edge:edge + 1])
        d2 = jnp.exp2(cumr[k2:k2 + 1, edge:edge + 1])
        dec = jnp.where(row_lo, jnp.broadcast_to(d1, row_lo.shape), jnp.broadcast_to(d2, row_lo.shape))
        state_ref[direction, :, cols] = s_prev * dec + _dot(bw, rhs2)


def _ssd_kernel(*refs, n_chunks, with_y):
    if with_y:
        (x_ref, b_ref, c_ref, z_ref, cumc_ref, cumr_ref, dtr_ref, wr_ref, dskip_ref, s0_ref,
         y_out_ref, sfin_ref, state_ref, y_ref) = refs
    else:
        x_ref, b_ref, cumr_ref, wr_ref, s0_ref, sfin_ref, state_ref = refs
        c_ref = z_ref = cumc_ref = dtr_ref = y_ref = None
    state_ref[...] = s0_ref[:, 0, 0]
    args = (x_ref, b_ref, c_ref, cumc_ref, cumr_ref, dtr_ref, wr_ref, y_ref, state_ref)

    def both(c, carry):
        _ssd_chunk(c, 0, with_y, *args)
        _ssd_chunk(n_chunks - 1 - c, 1, with_y, *args)
        return carry

    lax.fori_loop(0, n_chunks, both, 0)
    sfin_ref[:, 0, 0] = state_ref[...]

    if with_y:
        def gate(c, carry):
            rows = pl.ds(pl.multiple_of(c * SSD_CHUNK, SSD_CHUNK), SSD_CHUNK)
            y = y_ref[0, rows, :] + y_ref[1, rows, :] + x_ref[0, rows, :].astype(F32) * dskip_ref[0]
            y_out_ref[0, rows, :] = (y * _silu(z_ref[0, rows, :].astype(F32))).astype(y_out_ref.dtype)
            return carry

        lax.fori_loop(0, n_chunks, gate, 0)


def ssd_scan(xbc, prep, d_skip, s0, z=None):
    nb, t, _ = xbc.shape
    g, gw = N_BC_GROUPS, 2 * HEADS_PER_GROUP
    cumc, cumr, dtr, wr = prep
    with_y = z is not None
    b_off = D_INNER // D_STATE
    c_off = (D_INNER + GN) // D_STATE
    x_spec = pl.BlockSpec((1, t, GROUP_WIDTH), lambda b, gi: (b, 0, gi))
    b_spec = pl.BlockSpec((1, t, D_STATE), lambda b, gi: (b, 0, b_off + gi))
    c_spec = pl.BlockSpec((1, t, D_STATE), lambda b, gi: (b, 0, c_off + gi))
    col_spec = pl.BlockSpec((1, 1, t, gw), lambda b, gi: (b, gi, 0, 0))
    row_spec = pl.BlockSpec((1, 1, gw, t), lambda b, gi: (b, gi, 0, 0))
    s_spec = pl.BlockSpec((2, 1, 1, D_STATE, GROUP_WIDTH), lambda b, gi: (0, b, gi, 0, 0))
    s_shape = jax.ShapeDtypeStruct(s0.shape, F32)
    state = pltpu.VMEM((2, D_STATE, GROUP_WIDTH), F32)
    kern = functools.partial(_ssd_kernel, n_chunks=t // SSD_CHUNK, with_y=with_y)
    if not with_y:
        return pl.pallas_call(
            kern,
            grid=(nb, g),
            in_specs=[x_spec, b_spec, row_spec, row_spec, s_spec],
            out_specs=s_spec,
            out_shape=s_shape,
            scratch_shapes=[state],
            compiler_params=_params("arbitrary", "arbitrary"),
            name="ssd_states",
        )(xbc, xbc, cumr, wr, s0)
    dskip = jnp.repeat(d_skip, HEADDIM).reshape(g, 1, GROUP_WIDTH)
    return pl.pallas_call(
        kern,
        grid=(nb, g),
        in_specs=[x_spec, b_spec, c_spec, x_spec, col_spec, row_spec, row_spec, row_spec,
                  pl.BlockSpec((1, 1, GROUP_WIDTH), lambda b, gi: (gi, 0, 0)), s_spec],
        out_specs=[x_spec, s_spec],
        out_shape=[jax.ShapeDtypeStruct((nb, t, D_INNER), BF16), s_shape],
        scratch_shapes=[state, pltpu.VMEM((2, t, GROUP_WIDTH), F32)],
        compiler_params=_params("arbitrary", "arbitrary"),
        name="ssd_scan",
    )(xbc, xbc, xbc, z, cumc, cumr, dtr, wr, dskip, s0)


def _group_major(v):
    return v.reshape(2, N_BC_GROUPS, HEADS_PER_GROUP).transpose(1, 0, 2).reshape(1, 2 * N_HEADS)


def ssd_project(xn, w_in, w_dt, conv_w, conv_b, dt_bias, a, layer, with_z):
    nb, t, _ = xn.shape
    m = nb * t
    xf = xn.reshape(m, D_MODEL)
    tm = min(m, 512)
    conv_cols = CONV_DIM if with_z else D_INNER + GN
    (xbc,) = fused_matmul([xf], w_in, layer, D_INNER, conv_cols, tm, 1024, _epi_cast, [], [], [BF16], "ssd_in_xbc")
    (dt_raw,) = fused_matmul([xf], w_dt, 0, 0, 2 * N_HEADS, tm, 2 * N_HEADS, _epi_cast, [], [], [F32], "ssd_in_dt")
    xbc = dwconv_silu(xbc.reshape(nb, t, conv_cols), conv_w, conv_b, layer)
    prep = ssd_prep(dt_raw.reshape(nb, t, 2 * N_HEADS), _group_major(dt_bias), _group_major(a))
    z = None
    if with_z:
        (z,) = fused_matmul([xf], w_in, layer, 0, D_INNER, tm, 1024, _epi_cast, [], [], [BF16], "ssd_in_z")
        z = z.reshape(nb, t, D_INNER)
    return xbc, prep, z


def ssd_mix(xn, xc, h, gate, w_in, conv_w, conv_b, dt_bias, a_log, d_skip, norm_g, w_out, layer):
    nb, t, _ = xn.shape
    m = nb * t
    a = -jnp.exp(a_log[layer].astype(F32))
    w_dt = w_in[layer, :, D_INNER + CONV_DIM:].reshape(D_MODEL, 2, N_BC_GROUPS, HEADS_PER_GROUP)
    w_dt = w_dt.transpose(0, 2, 1, 3).reshape(1, D_MODEL, 2 * N_HEADS)
    zero = jnp.zeros((2, nb, N_BC_GROUPS, D_STATE, GROUP_WIDTH), F32)
    xbc_c, prep_c, _ = ssd_project(xc, w_in, w_dt, conv_w, conv_b, dt_bias[layer], a, layer, False)
    s_ctx = ssd_scan(xbc_c, prep_c, d_skip[layer], zero)
    xbc, prep, z = ssd_project(xn, w_in, w_dt, conv_w, conv_b, dt_bias[layer], a, layer, True)
    y, _ = ssd_scan(xbc, prep, d_skip[layer], s_ctx, z)
    hf = h.reshape(m, D_MODEL)
    tm, tn = 512, 512
    (out,) = fused_matmul([y.reshape(m, D_INNER)], w_out, layer, 0, D_MODEL, tm, tn, _epi_residual_rownorm,
                          [hf, gate], _residual_specs(hf, gate, t, tm, tn), [F32], "ssd_out",
                          w_row_scale=norm_g.reshape(norm_g.shape[0], D_INNER, 1))
    return out.reshape(nb, t, D_MODEL)


def _route_kernel(p_ref, tri_ref, pos_ref, post_ref, gate_ref, *, cap):
    probs = p_ref[0]
    t = probs.shape[0]
    pe = probs.T[:N_EXPERTS, :]
    bits = pltpu.bitcast(pe, jnp.int32)

    def step(i, v):
        cand = v | jnp.left_shift(jnp.int32(1), 30 - i)
        cnt = jnp.sum((bits >= cand).astype(F32), axis=1, keepdims=True)
        return jnp.where(cnt >= cap, cand, v)

    thr = lax.fori_loop(0, 31, step, jnp.zeros((N_EXPERTS, 1), jnp.int32))
    gt = bits > thr
    eq = bits == thr
    need = cap - jnp.sum(gt.astype(F32), axis=1, keepdims=True)
    tri = tri_ref[...]
    eq_rank = _dot(eq.astype(BF16), tri)
    sel = gt | (eq & (eq_rank <= need))
    pos = _dot(sel.astype(BF16), tri) - 1.0
    pos = jnp.where(sel, pos, -1.0)
    pos_ref[0] = pos
    pos_full = jnp.concatenate([pos, jnp.full((LANES - N_EXPERTS, t), -1.0, F32)], axis=0)
    pos_t = pos_full.T
    post_ref[0] = pos_t
    gate = jnp.where(pos_t >= 0.0, probs, 0.0)
    hi = gate.astype(BF16).astype(F32)
    mid = (gate - hi).astype(BF16).astype(F32)
    lo = (gate - hi - mid).astype(BF16).astype(F32)
    gate_ref[0] = (hi + pltpu.roll(mid, N_EXPERTS, 1) + pltpu.roll(lo, 2 * N_EXPERTS, 1)).astype(BF16)


def route(probs, cap):
    nb, t, _ = probs.shape
    idx = np.arange(t)
    tri = jnp.asarray((idx[:, None] <= idx[None, :]).astype(np.float32), dtype=BF16)
    kern = functools.partial(_route_kernel, cap=cap)
    return pl.pallas_call(
        kern,
        grid=(nb,),
        in_specs=[
            pl.BlockSpec((1, t, LANES), lambda b: (b, 0, 0)),
            pl.BlockSpec((t, t), lambda b: (0, 0)),
        ],
        out_specs=[
            pl.BlockSpec((1, N_EXPERTS, t), lambda b: (b, 0, 0)),
            pl.BlockSpec((1, t, LANES), lambda b: (b, 0, 0)),
            pl.BlockSpec((1, t, LANES), lambda b: (b, 0, 0)),
        ],
        out_shape=[
            jax.ShapeDtypeStruct((nb, N_EXPERTS, t), F32),
            jax.ShapeDtypeStruct((nb, t, LANES), F32),
            jax.ShapeDtypeStruct((nb, t, LANES), BF16),
        ],
        compiler_params=_params("arbitrary"),
        name="route",
    )(probs, tri)


def _gather_kernel(pos_ref, x_ref, gate_ref, *rest, cap):
    o_ref, gs_ref = rest[-2:]
    e = pl.program_id(1)
    pos = pos_ref[0, 0]
    t = pos.shape[1]
    slot = lax.broadcasted_iota(jnp.int32, (cap, t), 0).astype(F32)
    onehot = (slot == pos).astype(BF16)
    o_ref[0] = _dot(onehot, x_ref[0]).astype(o_ref.dtype)
    pieces = _dot(onehot, gate_ref[0])
    lane = lax.broadcasted_iota(jnp.int32, pieces.shape, 1)
    mine = (lane < 3 * N_EXPERTS) & (lane % N_EXPERTS == e)
    gs_ref[0] = jnp.sum(jnp.where(mine, pieces, 0.0), axis=1, keepdims=True)


def gather_tokens(pos, xm, gate_split, cap, total_rows=None, row0=0, dest=None):
    nb, t, _ = xm.shape
    total_rows = nb * cap if total_rows is None else total_rows
    assert row0 % cap == 0
    blk0 = row0 // cap
    kern = functools.partial(_gather_kernel, cap=cap)
    in_specs = [
        pl.BlockSpec((1, 1, 1, t), lambda b, e: (b, e, 0, 0)),
        pl.BlockSpec((1, t, D_MODEL), lambda b, e: (b, 0, 0)),
        pl.BlockSpec((1, t, LANES), lambda b, e: (b, 0, 0)),
    ]
    args = [pos.reshape(nb, N_EXPERTS, 1, t), xm, gate_split]
    aliases = {}
    if dest is not None:
        in_specs += [pl.BlockSpec(memory_space=pl.ANY), pl.BlockSpec(memory_space=pl.ANY)]
        args += list(dest)
        aliases = {3: 0, 4: 1}
    return pl.pallas_call(
        kern,
        grid=(nb, N_EXPERTS),
        in_specs=in_specs,
        out_specs=[
            pl.BlockSpec((1, cap, D_MODEL), lambda b, e: (e, blk0 + b, 0)),
            pl.BlockSpec((1, cap, 1), lambda b, e: (e, blk0 + b, 0)),
        ],
        out_shape=[
            jax.ShapeDtypeStruct((N_EXPERTS, total_rows, D_MODEL), BF16),
            jax.ShapeDtypeStruct((N_EXPERTS, total_rows, 1), F32),
        ],
        input_output_aliases=aliases,
        compiler_params=_params("arbitrary", "arbitrary"),
        name="gather_tokens",
    )(*args)


def _ffn_kernel(x_ref, gs_ref, wg_ref, wu_ref, wd_ref, o_ref, hid_ref, *, n_tiles):
    s = pl.program_id(2)
    tf = wg_ref.shape[1]

    @pl.when(s < n_tiles)
    def _():
        x = x_ref[0]
        gate = _dot(x, wg_ref[...].astype(BF16))
        up = _dot(x, wu_ref[...].astype(BF16))
        hid_ref[:, pl.ds(pl.multiple_of(s * tf, tf), tf)] = (_silu(gate) * up).astype(BF16)

    @pl.when(s >= n_tiles)
    def _():
        y = _dot(hid_ref[...], wd_ref[...].astype(BF16))
        o_ref[0] = (y * gs_ref[0]).astype(o_ref.dtype)


def expert_ffn(xe, gate_slot, w_gate, w_up, w_down, layer):
    _, m, _ = xe.shape
    tm = m if m <= 1024 else m // 2
    assert m % tm == 0 and tm % BF16_SUBLANES == 0
    tf = 512
    nt = D_MODEL // tf
    up_tile = lambda e, i, s: (layer, e, 0, jnp.minimum(s, nt - 1))
    down_tile = lambda e, i, s: (layer, e, 0, jnp.maximum(s - nt, 0))
    return pl.pallas_call(
        functools.partial(_ffn_kernel, n_tiles=nt),
        grid=(N_EXPERTS, m // tm, 2 * nt),
        in_specs=[
            pl.BlockSpec((1, tm, D_MODEL), lambda e, i, s: (e, i, 0)),
            pl.BlockSpec((1, tm, 1), lambda e, i, s: (e, i, 0)),
            pl.BlockSpec((None, None, D_MODEL, tf), up_tile),
            pl.BlockSpec((None, None, D_MODEL, tf), up_tile),
            pl.BlockSpec((None, None, D_MODEL, tf), down_tile),
        ],
        out_specs=pl.BlockSpec((1, tm, tf), lambda e, i, s: (e, i, jnp.maximum(s - nt, 0))),
        out_shape=jax.ShapeDtypeStruct(xe.shape, BF16),
        scratch_shapes=[pltpu.VMEM((tm, D_MODEL), BF16)],
        compiler_params=_params("arbitrary", "arbitrary", "arbitrary"),
        name="expert_ffn",
    )(xe, gate_slot, w_gate, w_up, w_down)


def _combine_kernel(*refs, cap, final_norm):
    post_ref, y_ref, h_ref, g2_ref = refs[:4]
    o_ref = refs[-1]
    tq = post_ref.shape[1]
    slot = lax.broadcasted_iota(jnp.int32, (tq, cap), 1).astype(F32)
    pos_t = post_ref[0]
    acc = None
    for e in range(N_EXPERTS):
        onehot = (slot == pos_t[:, e:e + 1]).astype(BF16)
        part = _dot(onehot, y_ref[e])
        acc = part if acc is None else acc + part
    out = h_ref[0] + g2_ref[0] * acc
    if final_norm:
        out = out * lax.rsqrt(jnp.mean(out * out, axis=-1, keepdims=True) + EPS) * refs[4][...]
    o_ref[0] = out


def combine(pos_t, ye, h, g2, cap, final_g=None, row0=0):
    nb, t, _ = h.shape
    tq = min(t, 256)
    assert row0 % cap == 0
    blk0 = row0 // cap
    if g2.shape[0] == 1:
        g2_spec = pl.BlockSpec((1, 1, D_MODEL), lambda b, i: (0, 0, 0))
    else:
        g2_spec = pl.BlockSpec((1, 1, D_MODEL), lambda b, i: (b, 0, 0))
    in_specs = [
        pl.BlockSpec((1, tq, LANES), lambda b, i: (b, i, 0)),
        pl.BlockSpec((N_EXPERTS, cap, D_MODEL), lambda b, i: (0, blk0 + b, 0)),
        pl.BlockSpec((1, tq, D_MODEL), lambda b, i: (b, i, 0)),
        g2_spec,
    ]
    args = [pos_t, ye, h, g2]
    if final_g is not None:
        in_specs.append(pl.BlockSpec((1, D_MODEL), lambda b, i: (0, 0)))
        args.append(final_g)
    kern = functools.partial(_combine_kernel, cap=cap, final_norm=final_g is not None)
    return pl.pallas_call(
        kern,
        grid=(nb, t // tq),
        in_specs=in_specs,
        out_specs=pl.BlockSpec((1, tq, D_MODEL), lambda b, i: (b, i, 0)),
        out_shape=jax.ShapeDtypeStruct(h.shape, F32),
        compiler_params=_params("arbitrary", "arbitrary"),
        name="combine",
    )(*args)


def expert_choice_ffn(streams, g, w_router, w_gate, w_up, w_down, layer, final_g=None):
    wr = jnp.pad(w_router[layer], ((0, 0), (0, LANES - N_EXPERTS)))
    caps = [CAPACITY_FACTOR * h.shape[1] // N_EXPERTS for h, _, _, _ in streams]
    rows = [h.shape[0] * cap for (h, _, _, _), cap in zip(streams, caps)]
    row0 = [sum(rows[:k]) for k in range(len(rows))]
    routed, dest = [], None
    for (h, shift, scale, _), cap, r0 in zip(streams, caps, row0):
        xm, probs = modulate_router(h, g, shift, scale, wr)
        pos, pos_t, gate_split = route(probs, cap)
        dest = gather_tokens(pos, xm, gate_split, cap, sum(rows), r0, dest)
        routed.append(pos_t)
    ye = expert_ffn(*dest, w_gate, w_up, w_down, layer)
    return [combine(pos_t, ye, h, g2, cap, final_g if k == 0 else None, r0)
            for k, ((h, _, _, g2), pos_t, cap, r0) in enumerate(zip(streams, routed, caps, row0))]


def _sincos_2d(rows, cols, dim):
    quarter = dim // 4
    omega = 1.0 / (10000.0 ** (jnp.arange(quarter, dtype=F32) / quarter))
    r = jnp.repeat(jnp.arange(rows, dtype=F32), cols)[:, None] * omega
    cl = jnp.tile(jnp.arange(cols, dtype=F32), rows)[:, None] * omega
    return jnp.concatenate([jnp.sin(r), jnp.cos(r), jnp.sin(cl), jnp.cos(cl)], axis=-1)


def kernel(x, c, ctx, c_ctx, mod_w, mod_b, norm_g, final_g, ab_w_in, ab_w_out, gm_v_g, gm_w_s, gm_b_s,
           ssd_w_in, ssd_conv_w, ssd_conv_b, ssd_dt_bias, ssd_a_log, ssd_d, ssd_norm_g, ssd_w_out,
           moe_w_router, moe_w_gate, moe_w_up, moe_w_down):
    nb, n, _ = x.shape
    depth = mod_w.shape[0]
    pos = _sincos_2d(n // GRID_W, GRID_W, D_MODEL)
    h = x
    hc = ctx
    cond = jnp.concatenate([c, c_ctx[None], jnp.zeros((16 - nb - 1, D_MODEL), F32)], axis=0)
    gm_b_s_t = jnp.swapaxes(gm_b_s, 1, 2)
    for i in range(depth):
        last = i == depth - 1
        even = i % 2 == 0
        j = i // 2
        ctx_reaches_latent = (not last) or (not even)
        mod = adaln(cond, mod_w, mod_b, i).reshape(16, 6, 1, D_MODEL)
        sh1, sc1, g1, sh2, sc2, g2 = (mod[:nb, k] for k in range(6))
        csh1, csc1, cg1, csh2, csc2, cg2 = (mod[nb:nb + 1, k] for k in range(6))
        g_mix = norm_g[i, 0][None]
        g_ffn = norm_g[i, 1][None]
        if pos is not None and not even:
            h, pos = h + pos[None], None
        xn = modulate(h, g_mix, sh1, sc1, pos)
        if ctx_reaches_latent:
            xc = modulate(hc, g_mix, csh1, csc1)
        if even:
            h = mixer_ab(xn, h, g1, ab_w_in, ab_w_out, gm_v_g, gm_w_s, gm_b_s_t, j, pos)
            pos = None
            if not last:
                hc = mixer_ab(xc, hc, cg1, ab_w_in, ab_w_out, gm_v_g, gm_w_s, gm_b_s_t, j)
        else:
            assert last, "context outputs of a state-space layer are only needed before the last layer"
            h = ssd_mix(xn, xc, h, g1, ssd_w_in, ssd_conv_w, ssd_conv_b, ssd_dt_bias, ssd_a_log, ssd_d,
                        ssd_norm_g, ssd_w_out, j)
        streams = [(h, sh2, sc2, g2)]
        if not last:
            streams.append((hc, csh2, csc2, cg2))
        outs = expert_choice_ffn(streams, g_ffn, moe_w_router, moe_w_gate, moe_w_up, moe_w_down, i,
                                 final_g[None] if last else None)
        h = outs[0]
        if not last:
            hc = outs[1]
    return h
```

```python
import functools
import math

import jax
import jax.numpy as jnp
import numpy as np
from jax import lax
from jax.experimental import pallas as pl
from jax.experimental.pallas import tpu as pltpu

F32 = jnp.float32
BF16 = jnp.bfloat16

D_MODEL = 2048
GRID_W = 64
EPS = 1e-6

A_WIDTH = D_MODEL // 2
GROUP_DIM = 128
A_GROUPS = A_WIDTH // GROUP_DIM
CHUNK = 128
B_WIDTH = D_MODEL // 2

D_INNER = 2 * D_MODEL
HEADDIM = 64
N_HEADS = D_INNER // HEADDIM
D_STATE = 128
N_BC_GROUPS = 8
HEADS_PER_GROUP = N_HEADS // N_BC_GROUPS
D_CONV = 5
SSD_CHUNK = 128
GN = N_BC_GROUPS * D_STATE
CONV_DIM = D_INNER + 2 * GN
GROUP_WIDTH = HEADS_PER_GROUP * HEADDIM

N_EXPERTS = 16
CAPACITY_FACTOR = 2

LANES = 128
BF16_SUBLANES = 16
VMEM_LIMIT = 56 * 1024 * 1024
LOG2E = 1.4426950408889634


def _params(*sem):
    return pltpu.CompilerParams(dimension_semantics=sem, vmem_limit_bytes=VMEM_LIMIT)


def _silu(x):
    return x / (1.0 + jnp.exp(-x))


def _split_bf16(x):
    hi = x.astype(BF16)
    lo = (x - hi.astype(F32)).astype(BF16)
    return hi, lo


def _dot(a, b):
    return jnp.dot(a, b, preferred_element_type=F32)


def _adaln_kernel(c_ref, w_ref, b_ref, o_ref):
    a = _silu(c_ref[...]).astype(BF16)
    o_ref[...] = _dot(a, w_ref[...].astype(BF16)) + b_ref[...]


def adaln(cond, mod_w, mod_b, layer):
    rows = cond.shape[0]
    tn = 1024
    n = 6 * D_MODEL
    return pl.pallas_call(
        _adaln_kernel,
        grid=(n // tn,),
        in_specs=[
            pl.BlockSpec((rows, D_MODEL), lambda j: (0, 0)),
            pl.BlockSpec((None, D_MODEL, tn), lambda j: (layer, 0, j)),
            pl.BlockSpec((None, 1, tn), lambda j: (layer, 0, j)),
        ],
        out_specs=pl.BlockSpec((rows, tn), lambda j: (0, j)),
        out_shape=jax.ShapeDtypeStruct((rows, n), F32),
        compiler_params=_params("arbitrary"),
        name="adaln",
    )(cond, mod_w, mod_b.reshape(mod_b.shape[0], 1, n))


def _rms_modulate(x, g, shift, scale):
    ms = jnp.mean(x * x, axis=-1, keepdims=True)
    y = x * lax.rsqrt(ms + EPS) * g
    return y * (1.0 + scale) + shift


def _modulate_kernel(*refs, with_pos):
    if with_pos:
        x_ref, pos_ref, g_ref, sh_ref, sc_ref, o_ref = refs
        x = x_ref[0] + pos_ref[...]
    else:
        x_ref, g_ref, sh_ref, sc_ref, o_ref = refs
        x = x_ref[0]
    o_ref[0] = _rms_modulate(x, g_ref[...], sh_ref[0], sc_ref[0]).astype(o_ref.dtype)


def _row_vec_spec(vec, n_batch):
    if vec.shape[0] == 1:
        return pl.BlockSpec((1, 1, D_MODEL), lambda b, i: (0, 0, 0))
    assert vec.shape[0] == n_batch
    return pl.BlockSpec((1, 1, D_MODEL), lambda b, i: (b, 0, 0))


def modulate(x, g, shift, scale, pos=None):
    nb, t, _ = x.shape
    tm = min(t, 512)
    in_specs = [pl.BlockSpec((1, tm, D_MODEL), lambda b, i: (b, i, 0))]
    args = [x]
    if pos is not None:
        in_specs.append(pl.BlockSpec((tm, D_MODEL), lambda b, i: (i, 0)))
        args.append(pos)
    in_specs += [pl.BlockSpec((1, D_MODEL), lambda b, i: (0, 0)), _row_vec_spec(shift, nb), _row_vec_spec(scale, nb)]
    return pl.pallas_call(
        functools.partial(_modulate_kernel, with_pos=pos is not None),
        grid=(nb, t // tm),
        in_specs=in_specs,
        out_specs=pl.BlockSpec((1, tm, D_MODEL), lambda b, i: (b, i, 0)),
        out_shape=jax.ShapeDtypeStruct(x.shape, BF16),
        compiler_params=_params("arbitrary", "arbitrary"),
        name="modulate",
    )(*args, g, shift, scale)


def _modulate_router_kernel(x_ref, g_ref, sh_ref, sc_ref, wr_ref, o_ref, p_ref):
    xm = _rms_modulate(x_ref[0], g_ref[...], sh_ref[0], sc_ref[0])
    xh, xl = _split_bf16(xm)
    wh, wl = _split_bf16(wr_ref[...])
    logits = _dot(xh, wh) + _dot(xl, wh) + _dot(xh, wl)
    lane = lax.broadcasted_iota(jnp.int32, logits.shape, 1)
    logits = jnp.where(lane < N_EXPERTS, logits, -jnp.inf)
    e = jnp.exp(logits - jnp.max(logits, axis=-1, keepdims=True))
    p_ref[0] = e / jnp.sum(e, axis=-1, keepdims=True)
    o_ref[0] = xh


def modulate_router(x, g, shift, scale, w_router_padded):
    nb, t, _ = x.shape
    tm = min(t, 512)
    return pl.pallas_call(
        _modulate_router_kernel,
        grid=(nb, t // tm),
        in_specs=[
            pl.BlockSpec((1, tm, D_MODEL), lambda b, i: (b, i, 0)),
            pl.BlockSpec((1, D_MODEL), lambda b, i: (0, 0)),
            _row_vec_spec(shift, nb),
            _row_vec_spec(scale, nb),
            pl.BlockSpec((D_MODEL, LANES), lambda b, i: (0, 0)),
        ],
        out_specs=[
            pl.BlockSpec((1, tm, D_MODEL), lambda b, i: (b, i, 0)),
            pl.BlockSpec((1, tm, LANES), lambda b, i: (b, i, 0)),
        ],
        out_shape=[
            jax.ShapeDtypeStruct(x.shape, BF16),
            jax.ShapeDtypeStruct((nb, t, LANES), F32),
        ],
        compiler_params=_params("arbitrary", "arbitrary"),
        name="modulate_router",
    )(x, g, shift, scale, w_router_padded)


def _matmul_kernel(*refs, n_a, n_extra, n_out, epilogue, row_scaled):
    a_refs = refs[:n_a]
    w_ref = refs[n_a]
    n_w = 2 if row_scaled else 1
    extra = refs[n_a + n_w:n_a + n_w + n_extra]
    outs = refs[n_a + n_w + n_extra:n_a + n_w + n_extra + n_out]
    wbf_ref = refs[-1]

    @pl.when(pl.program_id(1) == 0)
    def _():
        if row_scaled:
            wbf_ref[...] = (w_ref[...] * refs[n_a + 1][...]).astype(BF16)
        else:
            wbf_ref[...] = w_ref[...].astype(BF16)

    acc = None
    off = 0
    for a_ref in a_refs:
        k = a_ref.shape[-1]
        part = _dot(a_ref[...], wbf_ref[off:off + k, :])
        acc = part if acc is None else acc + part
        off += k
    epilogue(acc, extra, outs, a_refs)


def fused_matmul(a_list, w, layer, col0, ncols, tm, tn, epilogue, extras, extra_specs, out_dtypes, name,
                 w_row_scale=None):
    m = a_list[0].shape[0]
    k_total = w.shape[1]
    assert sum(a.shape[1] for a in a_list) == k_total
    assert m % tm == 0 and ncols % tn == 0 and col0 % tn == 0
    jb = col0 // tn
    in_specs = [pl.BlockSpec((tm, a.shape[1]), lambda j, i: (i, 0)) for a in a_list]
    in_specs.append(pl.BlockSpec((None, k_total, tn), lambda j, i: (layer, 0, jb + j)))
    weights = [w]
    if w_row_scale is not None:
        in_specs.append(pl.BlockSpec((None, k_total, 1), lambda j, i: (layer, 0, 0)))
        weights.append(w_row_scale)
    in_specs.extend(extra_specs)
    out_specs = [pl.BlockSpec((tm, tn), lambda j, i: (i, j)) for _ in out_dtypes]
    out_shape = [jax.ShapeDtypeStruct((m, ncols), dt) for dt in out_dtypes]
    kern = functools.partial(_matmul_kernel, n_a=len(a_list), n_extra=len(extras), n_out=len(out_dtypes),
                             epilogue=epilogue, row_scaled=w_row_scale is not None)
    return pl.pallas_call(
        kern,
        grid=(ncols // tn, m // tm),
        in_specs=in_specs,
        out_specs=out_specs,
        out_shape=out_shape,
        scratch_shapes=[pltpu.VMEM((k_total, tn), BF16)],
        compiler_params=_params("arbitrary", "arbitrary"),
        name=name,
    )(*a_list, *weights, *extras)


def _epi_cast(acc, extra, outs, a_refs):
    outs[0][...] = acc.astype(outs[0].dtype)


def _epi_gelu(acc, extra, outs, a_refs):
    outs[0][...] = (0.5 * acc * (1.0 + lax.erf(acc * (1.0 / math.sqrt(2.0))))).astype(outs[0].dtype)


def _epi_dft(acc, extra, outs, a_refs):
    ch_ref, cl_ref, sh_ref, sl_ref = extra
    a_ref, b_ref = outs
    zh, zl = _split_bf16(acc)
    for g in range(acc.shape[1] // GROUP_DIM):
        sl = slice(g * GROUP_DIM, (g + 1) * GROUP_DIM)
        a_ref[:, sl] = (_dot(zh[:, sl], ch_ref[...]) + _dot(zl[:, sl], ch_ref[...])
                        + _dot(zh[:, sl], cl_ref[...])).astype(a_ref.dtype)
        b_ref[:, sl] = (_dot(zh[:, sl], sh_ref[...]) + _dot(zl[:, sl], sh_ref[...])
                        + _dot(zh[:, sl], sl_ref[...])).astype(b_ref.dtype)


def _epi_residual(acc, extra, outs, a_refs):
    h_ref, gate_ref = extra
    outs[0][...] = h_ref[...] + gate_ref[0] * acc


def _epi_residual_pos(acc, extra, outs, a_refs):
    x_ref, pos_ref, gate_ref = extra
    outs[0][...] = x_ref[...] + pos_ref[...] + gate_ref[0] * acc


def _epi_residual_rownorm(acc, extra, outs, a_refs):
    h_ref, gate_ref = extra
    a = a_refs[0][...].astype(F32)
    inv = lax.rsqrt(jnp.mean(a * a, axis=1, keepdims=True) + EPS)
    outs[0][...] = h_ref[...] + gate_ref[0] * (acc * inv)


def _gate_spec(gate, t, tm, tn):
    per_batch = t // tm
    if gate.shape[0] == 1:
        return pl.BlockSpec((1, 1, tn), lambda j, i: (0, 0, j))
    return pl.BlockSpec((1, 1, tn), lambda j, i: (i // per_batch, 0, j))


def _residual_specs(h_flat, gate, t, tm, tn):
    return [pl.BlockSpec((tm, tn), lambda j, i: (i, j)), _gate_spec(gate, t, tm, tn)]


def _gmlp_kernel(u_ref, v_ref, vg_ref, ws_ref, bs_ref, o_ref):
    for c in range(v_ref.shape[1] // CHUNK):
        rows = slice(c * CHUNK, (c + 1) * CHUNK)
        v = v_ref[0, rows, :].astype(F32)
        vn = (v * lax.rsqrt(jnp.mean(v * v, axis=-1, keepdims=True) + EPS) * vg_ref[...]).astype(BF16)
        for g in range(A_GROUPS):
            sl = slice(g * GROUP_DIM, (g + 1) * GROUP_DIM)
            s = _dot(ws_ref[g].astype(BF16), vn[:, sl]) + bs_ref[:, g:g + 1]
            o_ref[0, rows, sl] = (u_ref[0, rows, sl].astype(F32) * s).astype(o_ref.dtype)


def chunk_gmlp(a, v_g, w_s, b_s_t, layer):
    nb, t, _ = a.shape
    rows = min(t, 512)
    return pl.pallas_call(
        _gmlp_kernel,
        grid=(nb, t // rows),
        in_specs=[
            pl.BlockSpec((1, rows, A_WIDTH), lambda b, c: (b, c, 0)),
            pl.BlockSpec((1, rows, A_WIDTH), lambda b, c: (b, c, 1)),
            pl.BlockSpec((None, 1, A_WIDTH), lambda b, c: (layer, 0, 0)),
            pl.BlockSpec((None, A_GROUPS, CHUNK, CHUNK), lambda b, c: (layer, 0, 0, 0)),
            pl.BlockSpec((None, CHUNK, A_GROUPS), lambda b, c: (layer, 0, 0)),
        ],
        out_specs=pl.BlockSpec((1, rows, A_WIDTH), lambda b, c: (b, c, 0)),
        out_shape=jax.ShapeDtypeStruct((nb, t, A_WIDTH), BF16),
        compiler_params=_params("arbitrary", "arbitrary"),
        name="chunk_gmlp",
    )(a, a, v_g.reshape(v_g.shape[0], 1, A_WIDTH), w_s, b_s_t)


def _dft_tables(n):
    k = np.arange(n, dtype=np.int64)
    ang = 2.0 * np.pi * ((k[:, None] * k[None, :]) % n).astype(np.float64) / n
    return np.cos(ang), np.sin(ang)


def _hi_lo(table):
    hi = jnp.asarray(table, dtype=F32).astype(BF16)
    lo = (jnp.asarray(table, dtype=F32) - hi.astype(F32)).astype(BF16)
    return hi, lo


def _token_dft_kernel(ct_ref, st_ref, a_ref, b_ref, o_ref, *, scale):
    y = _dot(ct_ref[...], a_ref[0]) - _dot(st_ref[...], b_ref[0])
    o_ref[0] = (y * scale).astype(o_ref.dtype)


def token_dft(a, b):
    nb, t, w = a.shape
    cos_t, sin_t = _dft_tables(t)
    ct = jnp.asarray(cos_t, dtype=F32).astype(BF16)
    st = jnp.asarray(sin_t, dtype=F32).astype(BF16)
    tm = min(t, 512)
    kern = functools.partial(_token_dft_kernel, scale=1.0 / math.sqrt(t * GROUP_DIM))
    return pl.pallas_call(
        kern,
        grid=(nb, t // tm),
        in_specs=[
            pl.BlockSpec((tm, t), lambda b_, i: (i, 0)),
            pl.BlockSpec((tm, t), lambda b_, i: (i, 0)),
            pl.BlockSpec((1, t, w), lambda b_, i: (b_, 0, 0)),
            pl.BlockSpec((1, t, w), lambda b_, i: (b_, 0, 0)),
        ],
        out_specs=pl.BlockSpec((1, tm, w), lambda b_, i: (b_, i, 0)),
        out_shape=jax.ShapeDtypeStruct((nb, t, w), BF16),
        compiler_params=_params("arbitrary", "arbitrary"),
        name="token_dft",
    )(ct, st, a, b)


def mixer_ab(xn, h, gate, ab_w_in, ab_w_out, gm_v_g, gm_w_s, gm_b_s_t, layer, pos=None):
    nb, t, _ = xn.shape
    m = nb * t
    xf = xn.reshape(m, D_MODEL)
    tm = min(m, 512)
    (a,) = fused_matmul([xf], ab_w_in, layer, 0, 2 * A_WIDTH, tm, 1024, _epi_gelu, [], [], [BF16], "ab_in_gelu")
    c128, s128 = _dft_tables(GROUP_DIM)
    tables = [*_hi_lo(c128), *_hi_lo(s128)]
    table_specs = [pl.BlockSpec((GROUP_DIM, GROUP_DIM), lambda j, i: (0, 0)) for _ in tables]
    za, zb = fused_matmul([xf], ab_w_in, layer, 2 * A_WIDTH, B_WIDTH, tm, 1024, _epi_dft, tables, table_specs,
                          [BF16, BF16], "ab_in_dft")
    ya = chunk_gmlp(a.reshape(nb, t, 2 * A_WIDTH), gm_v_g, gm_w_s, gm_b_s_t, layer)
    yb = token_dft(za.reshape(nb, t, B_WIDTH), zb.reshape(nb, t, B_WIDTH))
    hf = h.reshape(m, D_MODEL)
    tn = 1024
    if pos is None:
        epi, extras, specs = _epi_residual, [hf, gate], _residual_specs(hf, gate, t, tm, tn)
    else:
        per_batch = t // tm
        epi, extras = _epi_residual_pos, [hf, pos, gate]
        specs = [pl.BlockSpec((tm, tn), lambda j, i: (i, j)), pl.BlockSpec((tm, tn), lambda j, i: (i % per_batch, j)),
                 _gate_spec(gate, t, tm, tn)]
    (out,) = fused_matmul([ya.reshape(m, A_WIDTH), yb.reshape(m, B_WIDTH)], ab_w_out, layer, 0, D_MODEL, tm, tn,
                          epi, extras, specs, [F32], "ab_out")
    return out.reshape(nb, t, D_MODEL)


def _conv_kernel(prev_ref, main_ref, next_ref, w_ref, b_ref, o_ref, *, rows):
    r = pl.program_id(1)
    last = pl.num_programs(1) - 1
    prev = jnp.where(r > 0, prev_ref[0].astype(F32), 0.0)
    nxt = jnp.where(r < last, next_ref[0].astype(F32), 0.0)
    full = jnp.concatenate([prev, main_ref[0].astype(F32), nxt], axis=0)
    n = full.shape[0]
    w = w_ref[...]
    acc = None
    for k in range(D_CONV):
        shift = (D_CONV // 2 - k) % n
        term = (full if shift == 0 else pltpu.roll(full, shift, 0)) * w[k:k + 1, :]
        acc = term if acc is None else acc + term
    y = acc[BF16_SUBLANES:BF16_SUBLANES + rows, :] + b_ref[...]
    o_ref[0] = _silu(y).astype(o_ref.dtype)


def dwconv_silu(z, conv_w, conv_b, layer):
    nb, t, ch = z.shape
    rows = min(t, 512)
    tc = 1024
    halo = BF16_SUBLANES
    per = rows // halo
    n_halo = t // halo
    kern = functools.partial(_conv_kernel, rows=rows)
    return pl.pallas_call(
        kern,
        grid=(nb, t // rows, ch // tc),
        in_specs=[
            pl.BlockSpec((1, halo, tc), lambda b, r, c: (b, jnp.maximum(r * per - 1, 0), c)),
            pl.BlockSpec((1, rows, tc), lambda b, r, c: (b, r, c)),
            pl.BlockSpec((1, halo, tc), lambda b, r, c: (b, jnp.minimum((r + 1) * per, n_halo - 1), c)),
            pl.BlockSpec((None, D_CONV, tc), lambda b, r, c: (layer, 0, c)),
            pl.BlockSpec((None, 1, tc), lambda b, r, c: (layer, 0, c)),
        ],
        out_specs=pl.BlockSpec((1, rows, tc), lambda b, r, c: (b, r, c)),
        out_shape=jax.ShapeDtypeStruct(z.shape, BF16),
        compiler_params=_params("arbitrary", "arbitrary", "arbitrary"),
        name="dwconv_silu",
    )(z, z, z, conv_w, conv_b.reshape(conv_b.shape[0], 1, conv_b.shape[1]))


def _softplus(x):
    return jnp.maximum(x, 0.0) + jnp.log(1.0 + jnp.exp(-jnp.abs(x)))


def _prefix_sum(x, axis):
    n = x.shape[axis]
    idx = lax.broadcasted_iota(jnp.int32, x.shape, axis)
    s = 1
    while s < n:
        x = x + jnp.where(idx >= s, pltpu.roll(x, s, axis), 0.0)
        s *= 2
    return x


def _ssd_prep_kernel(raw_ref, bias_ref, a_ref, cumc_ref, cumr_ref, dtr_ref, wr_ref):
    L = SSD_CHUNK
    gw = 2 * HEADS_PER_GROUP
    lane = lax.broadcasted_iota(jnp.int32, (L, raw_ref.shape[2]), 1)
    is_bwd = (lane % gw) >= HEADS_PER_GROUP

    def chunk(c, carry):
        rows = pl.ds(pl.multiple_of(c * L, L), L)
        dt = _softplus(raw_ref[0, rows, :] + bias_ref[...])
        la = dt * (a_ref[...] * LOG2E)
        cum = _prefix_sum(la, 0)
        tot = cum[L - 1:L, :]
        cum = jnp.where(is_bwd, tot - cum + la, cum)
        w = jnp.exp2(tot - cum) * dt
        cum_t = cum.T
        dt_t = dt.T
        w_t = w.T
        for g in range(N_BC_GROUPS):
            ls = slice(g * gw, (g + 1) * gw)
            cumc_ref[0, g, rows, :] = cum[:, ls]
            cumr_ref[0, g, :, rows] = cum_t[ls, :]
            dtr_ref[0, g, :, rows] = dt_t[ls, :]
            wr_ref[0, g, :, rows] = w_t[ls, :]
        return carry

    lax.fori_loop(0, raw_ref.shape[1] // L, chunk, 0)


def ssd_prep(dt_raw, bias, a):
    nb, t, n = dt_raw.shape
    g, gw = N_BC_GROUPS, 2 * HEADS_PER_GROUP
    col = jax.ShapeDtypeStruct((nb, g, t, gw), F32)
    row = jax.ShapeDtypeStruct((nb, g, gw, t), F32)
    col_spec = pl.BlockSpec((1, g, t, gw), lambda b: (b, 0, 0, 0))
    row_spec = pl.BlockSpec((1, g, gw, t), lambda b: (b, 0, 0, 0))
    return pl.pallas_call(
        _ssd_prep_kernel,
        grid=(nb,),
        in_specs=[
            pl.BlockSpec((1, t, n), lambda b: (b, 0, 0)),
            pl.BlockSpec((1, n), lambda b: (0, 0)),
            pl.BlockSpec((1, n), lambda b: (0, 0)),
        ],
        out_specs=[col_spec, row_spec, row_spec, row_spec],
        out_shape=[col, row, row, row],
        compiler_params=_params("arbitrary"),
        name="ssd_prep",
    )(dt_raw, bias, a)


def _ssd_chunk(c, direction, with_y, x_ref, b_ref, c_ref, cumc_ref, cumr_ref, dtr_ref, wr_ref, y_ref, state_ref):
    L = SSD_CHUNK
    hp = HEADS_PER_GROUP
    h0 = direction * hp
    rows = pl.ds(pl.multiple_of(c * L, L), L)
    lane_lo = lax.broadcasted_iota(jnp.int32, (L, 2 * HEADDIM), 1) < HEADDIM
    row_lo = lane_lo[:1]
    edge = L - 1 if direction == 0 else 0

    bm = b_ref[0, rows, :]
    bm_t = bm.astype(F32).T
    cumr = cumr_ref[0, 0, :, rows]
    wr = wr_ref[0, 0, :, rows]

    if with_y:
        cm = c_ref[0, rows, :]
        cb = lax.dot_general(cm, bm, (((1,), (1,)), ((), ())), preferred_element_type=F32)
        ii = lax.broadcasted_iota(jnp.int32, (L, L), 0)
        jj = lax.broadcasted_iota(jnp.int32, (L, L), 1)
        causal = (jj <= ii) if direction == 0 else (jj >= ii)
        cbm = jnp.where(causal, cb, 0.0)
        cumc = cumc_ref[0, 0, rows, :]
        dtr = dtr_ref[0, 0, :, rows]

    for p in range(hp // 2):
        k1, k2 = h0 + 2 * p, h0 + 2 * p + 1
        cols = slice(p * 2 * HEADDIM, (p + 1) * 2 * HEADDIM)
        s_prev = state_ref[direction, :, cols]
        xp = x_ref[0, rows, cols]
        zero = jnp.zeros_like(xp)
        rhs2 = jnp.concatenate([jnp.where(lane_lo, xp, zero), jnp.where(lane_lo, zero, xp)], axis=0)
        if with_y:
            ms, cum_b = [], []
            for k in (k1, k2):
                cum_b.append(jnp.broadcast_to(cumc[:, k:k + 1], (L, L)))
                seg = jnp.minimum(cum_b[-1] - cumr[k:k + 1, :], 0.0)
                ms.append((cbm * jnp.exp2(seg) * dtr[k:k + 1, :]).astype(BF16))
            edec = jnp.exp2(jnp.where(lane_lo, cum_b[0], cum_b[1]))
            y_ref[direction, rows, cols] = (_dot(jnp.concatenate(ms, axis=1), rhs2)
                                            + _dot(cm, s_prev.astype(BF16)) * edec)
        bw = jnp.concatenate([(bm_t * wr[k1:k1 + 1, :]).astype(BF16), (bm_t * wr[k2:k2 + 1, :]).astype(BF16)], axis=1)
        d1 = jnp.exp2(cumr[k1:k1 + 1, edge:edge + 1])
        d2 = jnp.exp2(cumr[k2:k2 + 1, edge:edge + 1])
        dec = jnp.where(row_lo, jnp.broadcast_to(d1, row_lo.shape), jnp.broadcast_to(d2, row_lo.shape))
        state_ref[direction, :, cols] = s_prev * dec + _dot(bw, rhs2)


def _ssd_kernel(*refs, n_chunks, with_y):
    if with_y:
        (x_ref, b_ref, c_ref, z_ref, cumc_ref, cumr_ref, dtr_ref, wr_ref, dskip_ref, s0_ref,
         y_out_ref, sfin_ref, state_ref, y_ref) = refs
        state_ref[...] = s0_ref[:, 0, 0]
    else:
        x_ref, b_ref, cumr_ref, wr_ref, sfin_ref, state_ref = refs
        c_ref = z_ref = cumc_ref = dtr_ref = y_ref = None
        state_ref[...] = jnp.zeros_like(state_ref)
    args =(x_ref, b_ref, c_ref, cumc_ref, cumr_ref, dtr_ref, wr_ref, y_ref, state_ref)

    def both(c, carry):
        _ssd_chunk(c, 0, with_y, *args)
        _ssd_chunk(n_chunks - 1 - c, 1, with_y, *args)
        return carry

    lax.fori_loop(0, n_chunks, both, 0, unroll=min(4, n_chunks))
    sfin_ref[:, 0, 0] = state_ref[...]

    if with_y:
        def gate(c, carry):
            rows = pl.ds(pl.multiple_of(c * SSD_CHUNK, SSD_CHUNK), SSD_CHUNK)
            y = y_ref[0, rows, :] + y_ref[1, rows, :] + x_ref[0, rows, :].astype(F32) * dskip_ref[0]
            y_out_ref[0, rows, :] = (y * _silu(z_ref[0, rows, :].astype(F32))).astype(y_out_ref.dtype)
            return carry

        lax.fori_loop(0, n_chunks, gate, 0)


def ssd_scan(xbc, prep, d_skip=None, s0=None, z=None):
    nb, t, _ = xbc.shape
    g, gw = N_BC_GROUPS, 2 * HEADS_PER_GROUP
    cumc, cumr, dtr, wr = prep
    with_y = z is not None
    b_off = D_INNER // D_STATE
    c_off = (D_INNER + GN) // D_STATE
    x_spec = pl.BlockSpec((1, t, GROUP_WIDTH), lambda b, gi: (b, 0, gi))
    b_spec = pl.BlockSpec((1, t, D_STATE), lambda b, gi: (b, 0, b_off + gi))
    c_spec = pl.BlockSpec((1, t, D_STATE), lambda b, gi: (b, 0, c_off + gi))
    col_spec = pl.BlockSpec((1, 1, t, gw), lambda b, gi: (b, gi, 0, 0))
    row_spec = pl.BlockSpec((1, 1, gw, t), lambda b, gi: (b, gi, 0, 0))
    s_spec = pl.BlockSpec((2, 1, 1, D_STATE, GROUP_WIDTH), lambda b, gi: (0, b, gi, 0, 0))
    s_shape = jax.ShapeDtypeStruct((2, nb, g, D_STATE, GROUP_WIDTH), F32)
    state = pltpu.VMEM((2, D_STATE, GROUP_WIDTH), F32)
    kern = functools.partial(_ssd_kernel, n_chunks=t // SSD_CHUNK, with_y=with_y)
    if not with_y:
        assert s0 is None and d_skip is None
        return pl.pallas_call(
            kern,
            grid=(nb, g),
            in_specs=[x_spec, b_spec, row_spec, row_spec],
            out_specs=s_spec,
            out_shape=s_shape,
            scratch_shapes=[state],
            compiler_params=_params("arbitrary", "arbitrary"),
            name="ssd_states",
        )(xbc, xbc, cumr, wr)
    dskip = jnp.repeat(d_skip, HEADDIM).reshape(g, 1, GROUP_WIDTH)
    return pl.pallas_call(
        kern,
        grid=(nb, g),
        in_specs=[x_spec, b_spec, c_spec, x_spec, col_spec, row_spec, row_spec, row_spec,
                  pl.BlockSpec((1, 1, GROUP_WIDTH), lambda b, gi: (gi, 0, 0)), s_spec],
        out_specs=[x_spec, s_spec],
        out_shape=[jax.ShapeDtypeStruct((nb, t, D_INNER), BF16), s_shape],
        scratch_shapes=[state, pltpu.VMEM((2, t, GROUP_WIDTH), F32)],
        compiler_params=_params("arbitrary", "arbitrary"),
        name="ssd_scan",
    )(xbc, xbc, xbc, z, cumc, cumr, dtr, wr, dskip, s0)


def _group_major(v):
    return v.reshape(2, N_BC_GROUPS, HEADS_PER_GROUP).transpose(1, 0, 2).reshape(1, 2 * N_HEADS)


def ssd_project(xn, w_in, w_dt, conv_w, conv_b, dt_bias, a, layer, with_z):
    nb, t, _ = xn.shape
    m = nb * t
    xf = xn.reshape(m, D_MODEL)
    tm = min(m, 512)
    conv_cols = CONV_DIM if with_z else D_INNER + GN
    (xbc,) = fused_matmul([xf], w_in, layer, D_INNER, conv_cols, tm, 1024, _epi_cast, [], [], [BF16], "ssd_in_xbc")
    (dt_raw,) = fused_matmul([xf], w_dt, 0, 0, 2 * N_HEADS, tm, 2 * N_HEADS, _epi_cast, [], [], [F32], "ssd_in_dt")
    xbc = dwconv_silu(xbc.reshape(nb, t, conv_cols), conv_w, conv_b, layer)
    prep = ssd_prep(dt_raw.reshape(nb, t, 2 * N_HEADS), _group_major(dt_bias), _group_major(a))
    z = None
    if with_z:
        (z,) = fused_matmul([xf], w_in, layer, 0, D_INNER, tm, 1024, _epi_cast, [], [], [BF16], "ssd_in_z")
        z = z.reshape(nb, t, D_INNER)
    return xbc, prep, z


def ssd_mix(xn, xc, h, gate, w_in, conv_w, conv_b, dt_bias, a_log, d_skip, norm_g, w_out, layer):
    nb, t, _ = xn.shape
    m = nb * t
    a = -jnp.exp(a_log[layer].astype(F32))
    w_dt = w_in[layer, :, D_INNER + CONV_DIM:].reshape(D_MODEL, 2, N_BC_GROUPS, HEADS_PER_GROUP)
    w_dt = w_dt.transpose(0, 2, 1, 3).reshape(1, D_MODEL, 2 * N_HEADS)
    xbc_c, prep_c, _ = ssd_project(xc, w_in, w_dt, conv_w, conv_b, dt_bias[layer], a, layer, False)
    s_ctx = ssd_scan(xbc_c, prep_c)
    xbc, prep, z = ssd_project(xn, w_in, w_dt, conv_w, conv_b, dt_bias[layer], a, layer, True)
    y, _ = ssd_scan(xbc, prep, d_skip[layer], s_ctx, z)
    hf = h.reshape(m, D_MODEL)
    tm, tn = 512, 512
    (out,) = fused_matmul([y.reshape(m, D_INNER)], w_out, layer, 0, D_MODEL, tm, tn, _epi_residual_rownorm,
                          [hf, gate], _residual_specs(hf, gate, t, tm, tn), [F32], "ssd_out",
                          w_row_scale=norm_g.reshape(norm_g.shape[0], D_INNER, 1))
    return out.reshape(nb, t, D_MODEL)


def _route_kernel(p_ref, tri_ref, pos_ref, post_ref, gate_ref, *, cap):
    probs = p_ref[0]
    t = probs.shape[0]
    pe = probs.T[:N_EXPERTS, :]
    bits = pltpu.bitcast(pe, jnp.int32)

    def step(i, v):
        cand = v | jnp.left_shift(jnp.int32(1), 30 - i)
        cnt = jnp.sum((bits >= cand).astype(F32), axis=1, keepdims=True)
        return jnp.where(cnt >= cap, cand, v)

    thr = lax.fori_loop(0, 31, step, jnp.zeros((N_EXPERTS, 1), jnp.int32))
    gt = bits > thr
    eq = bits == thr
    need = cap - jnp.sum(gt.astype(F32), axis=1, keepdims=True)
    tri = tri_ref[...]
    eq_rank = _dot(eq.astype(BF16), tri)
    sel = gt | (eq & (eq_rank <= need))
    pos = _dot(sel.astype(BF16), tri) - 1.0
    pos = jnp.where(sel, pos, -1.0)
    pos_ref[0] = pos
    pos_full = jnp.concatenate([pos, jnp.full((LANES - N_EXPERTS, t), -1.0, F32)], axis=0)
    pos_t = pos_full.T
    post_ref[0] = pos_t
    gate = jnp.where(pos_t >= 0.0, probs, 0.0)
    hi = gate.astype(BF16).astype(F32)
    mid = (gate - hi).astype(BF16).astype(F32)
    lo = (gate - hi - mid).astype(BF16).astype(F32)
    gate_ref[0] = (hi + pltpu.roll(mid, N_EXPERTS, 1) + pltpu.roll(lo, 2 * N_EXPERTS, 1)).astype(BF16)


def route(probs, cap):
    nb, t, _ = probs.shape
    idx = np.arange(t)
    tri = jnp.asarray((idx[:, None] <= idx[None, :]).astype(np.float32), dtype=BF16)
    kern = functools.partial(_route_kernel, cap=cap)
    return pl.pallas_call(
        kern,
        grid=(nb,),
        in_specs=[
            pl.BlockSpec((1, t, LANES), lambda b: (b, 0, 0)),
            pl.BlockSpec((t, t), lambda b: (0, 0)),
        ],
        out_specs=[
            pl.BlockSpec((1, N_EXPERTS, t), lambda b: (b, 0, 0)),
            pl.BlockSpec((1, t, LANES), lambda b: (b, 0, 0)),
            pl.BlockSpec((1, t, LANES), lambda b: (b, 0, 0)),
        ],
        out_shape=[
            jax.ShapeDtypeStruct((nb, N_EXPERTS, t), F32),
            jax.ShapeDtypeStruct((nb, t, LANES), F32),
            jax.ShapeDtypeStruct((nb, t, LANES), BF16),
        ],
        compiler_params=_params("arbitrary"),
        name="route",
    )(probs, tri)


def _gather_kernel(pos_ref, x_ref, gate_ref, *rest, cap):
    o_ref, gs_ref = rest[-2:]
    e = pl.program_id(1)
    pos = pos_ref[0, 0]
    t = pos.shape[1]
    slot = lax.broadcasted_iota(jnp.int32, (cap, t), 0).astype(F32)
    onehot = (slot == pos).astype(BF16)
    o_ref[0] = _dot(onehot, x_ref[0]).astype(o_ref.dtype)
    pieces = _dot(onehot, gate_ref[0])
    lane = lax.broadcasted_iota(jnp.int32, pieces.shape, 1)
    mine = (lane < 3 * N_EXPERTS) & (lane % N_EXPERTS == e)
    gs_ref[0] = jnp.sum(jnp.where(mine, pieces, 0.0), axis=1, keepdims=True)


def gather_tokens(pos, xm, gate_split, cap, total_rows=None, row0=0, dest=None):
    nb, t, _ = xm.shape
    total_rows = nb * cap if total_rows is None else total_rows
    assert row0 % cap == 0
    blk0 = row0 // cap
    kern = functools.partial(_gather_kernel, cap=cap)
    in_specs = [
        pl.BlockSpec((1, 1, 1, t), lambda b, e: (b, e, 0, 0)),
        pl.BlockSpec((1, t, D_MODEL), lambda b, e: (b, 0, 0)),
        pl.BlockSpec((1, t, LANES), lambda b, e: (b, 0, 0)),
    ]
    args = [pos.reshape(nb, N_EXPERTS, 1, t), xm, gate_split]
    aliases = {}
    if dest is not None:
        in_specs += [pl.BlockSpec(memory_space=pl.ANY), pl.BlockSpec(memory_space=pl.ANY)]
        args += list(dest)
        aliases = {3: 0, 4: 1}
    return pl.pallas_call(
        kern,
        grid=(nb, N_EXPERTS),
        in_specs=in_specs,
        out_specs=[
            pl.BlockSpec((1, cap, D_MODEL), lambda b, e: (e, blk0 + b, 0)),
            pl.BlockSpec((1, cap, 1), lambda b, e: (e, blk0 + b, 0)),
        ],
        out_shape=[
            jax.ShapeDtypeStruct((N_EXPERTS, total_rows, D_MODEL), BF16),
            jax.ShapeDtypeStruct((N_EXPERTS, total_rows, 1), F32),
        ],
        input_output_aliases=aliases,
        compiler_params=_params("arbitrary", "arbitrary"),
        name="gather_tokens",
    )(*args)


def _ffn_kernel(x_ref, gs_ref, wg_ref, wu_ref, wd_ref, o_ref, acc_ref):
    f = pl.program_id(2)
    x = x_ref[0]
    gate = _dot(x, wg_ref[...].astype(BF16))
    up = _dot(x, wu_ref[...].astype(BF16))
    hid = (_silu(gate) * up).astype(BF16)

    @pl.when(f == 0)
    def _():
        acc_ref[...] = jnp.zeros_like(acc_ref)

    acc_ref[...] += _dot(hid, wd_ref[...].astype(BF16))

    @pl.when(f == pl.num_programs(2) - 1)
    def _():
        o_ref[0] = (acc_ref[...] * gs_ref[0]).astype(o_ref.dtype)


def expert_ffn(xe, gate_slot, w_gate, w_up, w_down, layer):
    _, m, _ = xe.shape
    tm = m if m <= 1024 else m // 2
    assert m % tm == 0 and tm % BF16_SUBLANES == 0
    tf = 256
    return pl.pallas_call(
        _ffn_kernel,
        grid=(N_EXPERTS, m // tm, D_MODEL // tf),
        in_specs=[
            pl.BlockSpec((1, tm, D_MODEL), lambda e, i, f: (e, i, 0)),
            pl.BlockSpec((1, tm, 1), lambda e, i, f: (e, i, 0)),
            pl.BlockSpec((None, None, D_MODEL, tf), lambda e, i, f: (layer, e, 0, f)),
            pl.BlockSpec((None, None, D_MODEL, tf), lambda e, i, f: (layer, e, 0, f)),
            pl.BlockSpec((None, None, tf, D_MODEL), lambda e, i, f: (layer, e, f, 0)),
        ],
        out_specs=pl.BlockSpec((1, tm, D_MODEL), lambda e, i, f: (e, i, 0)),
        out_shape=jax.ShapeDtypeStruct(xe.shape, BF16),
        scratch_shapes=[pltpu.VMEM((tm, D_MODEL), F32)],
        compiler_params=_params("arbitrary", "arbitrary", "arbitrary"),
        name="expert_ffn",
    )(xe, gate_slot, w_gate, w_up, w_down)


def _combine_kernel(*refs, cap, final_norm):
    post_ref, y_ref, h_ref, g2_ref = refs[:4]
    o_ref = refs[-1]
    tq = post_ref.shape[1]
    slot = lax.broadcasted_iota(jnp.int32, (tq, cap), 1).astype(F32)
    pos_t = post_ref[0]
    acc = None
    for e in range(N_EXPERTS):
        onehot = (slot == pos_t[:, e:e + 1]).astype(BF16)
        part = _dot(onehot, y_ref[e])
        acc = part if acc is None else acc + part
    out = h_ref[0] + g2_ref[0] * acc
    if final_norm:
        out = out * lax.rsqrt(jnp.mean(out * out, axis=-1, keepdims=True) + EPS) * refs[4][...]
    o_ref[0] = out


def combine(pos_t, ye, h, g2, cap, final_g=None, row0=0):
    nb, t, _ = h.shape
    tq = min(t, 256)
    assert row0 % cap == 0
    blk0 = row0 // cap
    if g2.shape[0] == 1:
        g2_spec = pl.BlockSpec((1, 1, D_MODEL), lambda b, i: (0, 0, 0))
    else:
        g2_spec = pl.BlockSpec((1, 1, D_MODEL), lambda b, i: (b, 0, 0))
    in_specs = [
        pl.BlockSpec((1, tq, LANES), lambda b, i: (b, i, 0)),
        pl.BlockSpec((N_EXPERTS, cap, D_MODEL), lambda b, i: (0, blk0 + b, 0)),
        pl.BlockSpec((1, tq, D_MODEL), lambda b, i: (b, i, 0)),
        g2_spec,
    ]
    args = [pos_t, ye, h, g2]
    if final_g is not None:
        in_specs.append(pl.BlockSpec((1, D_MODEL), lambda b, i: (0, 0)))
        args.append(final_g)
    kern = functools.partial(_combine_kernel, cap=cap, final_norm=final_g is not None)
    return pl.pallas_call(
        kern,
        grid=(nb, t // tq),
        in_specs=in_specs,
        out_specs=pl.BlockSpec((1, tq, D_MODEL), lambda b, i: (b, i, 0)),
        out_shape=jax.ShapeDtypeStruct(h.shape, F32),
        compiler_params=_params("arbitrary", "arbitrary"),
        name="combine",
    )(*args)


def expert_choice_ffn(streams, g, w_router, w_gate, w_up, w_down, layer, final_g=None):
    wr = jnp.pad(w_router[layer], ((0, 0), (0, LANES - N_EXPERTS)))
    caps = [CAPACITY_FACTOR * h.shape[1] // N_EXPERTS for h, _, _, _ in streams]
    rows = [h.shape[0] * cap for (h, _, _, _), cap in zip(streams, caps)]
    row0 = [sum(rows[:k]) for k in range(len(rows))]
    routed, dest = [], None
    for (h, shift, scale, _), cap, r0 in zip(streams, caps, row0):
        xm, probs = modulate_router(h, g, shift, scale, wr)
        pos, pos_t, gate_split = route(probs, cap)
        dest = gather_tokens(pos, xm, gate_split, cap, sum(rows), r0, dest)
        routed.append(pos_t)
    ye = expert_ffn(*dest, w_gate, w_up, w_down, layer)
    return [combine(pos_t, ye, h, g2, cap, final_g if k == 0 else None, r0)
            for k, ((h, _, _, g2), pos_t, cap, r0) in enumerate(zip(streams, routed, caps, row0))]


def _sincos_2d(rows, cols, dim):
    quarter = dim // 4
    omega = 1.0 / (10000.0 ** (np.arange(quarter, dtype=np.float64) / quarter))
    r = np.repeat(np.arange(rows, dtype=np.float64), cols)[:, None] * omega
    cl = np.tile(np.arange(cols, dtype=np.float64), rows)[:, None] * omega
    return jnp.asarray(np.concatenate([np.sin(r), np.cos(r), np.sin(cl), np.cos(cl)], axis=-1), dtype=F32)


def kernel(x, c, ctx, c_ctx, mod_w, mod_b, norm_g, final_g, ab_w_in, ab_w_out, gm_v_g, gm_w_s, gm_b_s,
           ssd_w_in, ssd_conv_w, ssd_conv_b, ssd_dt_bias, ssd_a_log, ssd_d, ssd_norm_g, ssd_w_out,
           moe_w_router, moe_w_gate, moe_w_up, moe_w_down):
    nb, n, _ = x.shape
    depth = mod_w.shape[0]
    pos = _sincos_2d(n // GRID_W, GRID_W, D_MODEL)
    h = x
    hc = ctx
    cond = jnp.concatenate([c, c_ctx[None], jnp.zeros((16 - nb - 1, D_MODEL), F32)], axis=0)
    gm_b_s_t = jnp.swapaxes(gm_b_s, 1, 2)
    for i in range(depth):
        last = i == depth - 1
        even = i % 2 == 0
        j = i // 2
        ctx_reaches_latent = (not last) or (not even)
        mod = adaln(cond, mod_w, mod_b, i).reshape(16, 6, 1, D_MODEL)
        sh1, sc1, g1, sh2, sc2, g2 = (mod[:nb, k] for k in range(6))
        csh1, csc1, cg1, csh2, csc2, cg2 = (mod[nb:nb + 1, k] for k in range(6))
        g_mix = norm_g[i, 0][None]
        g_ffn = norm_g[i, 1][None]
        if pos is not None and not even:
            h, pos = h + pos[None], None
        xn = modulate(h, g_mix, sh1, sc1, pos)
        if ctx_reaches_latent:
            xc = modulate(hc, g_mix, csh1, csc1)
        if even:
            h = mixer_ab(xn, h, g1, ab_w_in, ab_w_out, gm_v_g, gm_w_s, gm_b_s_t, j, pos)
            pos = None
            if not last:
                hc = mixer_ab(xc, hc, cg1, ab_w_in, ab_w_out, gm_v_g, gm_w_s, gm_b_s_t, j)
        else:
            assert last, "context outputs of a state-space layer are only needed before the last layer"
            h = ssd_mix(xn, xc, h, g1, ssd_w_in, ssd_conv_w, ssd_conv_b, ssd_dt_bias, ssd_a_log, ssd_d,
                        ssd_norm_g, ssd_w_out, j)
        streams = [(h, sh2, sc2, g2)]
        if not last:
            streams.append((hc, csh2, csc2, cg2))
        outs = expert_choice_ffn(streams, g_ffn, moe_w_router, moe_w_gate, moe_w_up, moe_w_down, i,
                                 final_g[None] if last else None)
        h = outs[0]
        if not last:
            hc = outs[1]
    return h
```

```python
import functools
import math

import jax
import jax.numpy as jnp
import numpy as np
from jax import lax
from jax.experimental import pallas as pl
from jax.experimental.pallas import tpu as pltpu

F32 = jnp.float32
BF16 = jnp.bfloat16

D_MODEL = 2048
GRID_W = 64
EPS = 1e-6

A_WIDTH = D_MODEL // 2
GROUP_DIM = 128
A_GROUPS = A_WIDTH // GROUP_DIM
CHUNK = 128
B_WIDTH = D_MODEL // 2

D_INNER = 2 * D_MODEL
HEADDIM = 64
N_HEADS = D_INNER // HEADDIM
D_STATE = 128
N_BC_GROUPS = 8
HEADS_PER_GROUP = N_HEADS // N_BC_GROUPS
D_CONV = 5
SSD_CHUNK = 128
GN = N_BC_GROUPS * D_STATE
CONV_DIM = D_INNER + 2 * GN
GROUP_WIDTH = HEADS_PER_GROUP * HEADDIM

N_EXPERTS = 16
CAPACITY_FACTOR = 2

LANES = 128
BF16_SUBLANES = 16
VMEM_LIMIT = 56 * 1024 * 1024
LOG2E = 1.4426950408889634


def _params(*sem):
    return pltpu.CompilerParams(dimension_semantics=sem, vmem_limit_bytes=VMEM_LIMIT)


def _silu(x):
    return x / (1.0 + jnp.exp(-x))


def _split_bf16(x):
    hi = x.astype(BF16)
    lo = (x - hi.astype(F32)).astype(BF16)
    return hi, lo


def _dot(a, b):
    return jnp.dot(a, b, preferred_element_type=F32)


def _adaln_kernel(c_ref, w_ref, b_ref, o_ref):
    a = _silu(c_ref[...]).astype(BF16)
    o_ref[...] = _dot(a, w_ref[...].astype(BF16)) + b_ref[...]


def adaln(cond, mod_w, mod_b, layer):
    rows = cond.shape[0]
    tn = 1024
    n = 6 * D_MODEL
    return pl.pallas_call(
        _adaln_kernel,
        grid=(n // tn,),
        in_specs=[
            pl.BlockSpec((rows, D_MODEL), lambda j: (0, 0)),
            pl.BlockSpec((None, D_MODEL, tn), lambda j: (layer, 0, j)),
            pl.BlockSpec((None, 1, tn), lambda j: (layer, 0, j)),
        ],
        out_specs=pl.BlockSpec((rows, tn), lambda j: (0, j)),
        out_shape=jax.ShapeDtypeStruct((rows, n), F32),
        compiler_params=_params("arbitrary"),
        name="adaln",
    )(cond, mod_w, mod_b.reshape(mod_b.shape[0], 1, n))


def _rms_modulate(x, g, shift, scale):
    ms = jnp.mean(x * x, axis=-1, keepdims=True)
    y = x * lax.rsqrt(ms + EPS) * g
    return y * (1.0 + scale) + shift


def _modulate_kernel(*refs, with_pos):
    if with_pos:
        x_ref, pos_ref, g_ref, sh_ref, sc_ref, o_ref = refs
        x = x_ref[0] + pos_ref[...]
    else:
        x_ref, g_ref, sh_ref, sc_ref, o_ref = refs
        x = x_ref[0]
    o_ref[0] = _rms_modulate(x, g_ref[...], sh_ref[0], sc_ref[0]).astype(o_ref.dtype)


def _row_vec_spec(vec, n_batch):
    if vec.shape[0] == 1:
        return pl.BlockSpec((1, 1, D_MODEL), lambda b, i: (0, 0, 0))
    assert vec.shape[0] == n_batch
    return pl.BlockSpec((1, 1, D_MODEL), lambda b, i: (b, 0, 0))


def modulate(x, g, shift, scale, pos=None):
    nb, t, _ = x.shape
    tm = min(t, 512)
    in_specs = [pl.BlockSpec((1, tm, D_MODEL), lambda b, i: (b, i, 0))]
    args = [x]
    if pos is not None:
        in_specs.append(pl.BlockSpec((tm, D_MODEL), lambda b, i: (i, 0)))
        args.append(pos)
    in_specs += [pl.BlockSpec((1, D_MODEL), lambda b, i: (0, 0)), _row_vec_spec(shift, nb), _row_vec_spec(scale, nb)]
    return pl.pallas_call(
        functools.partial(_modulate_kernel, with_pos=pos is not None),
        grid=(nb, t // tm),
        in_specs=in_specs,
        out_specs=pl.BlockSpec((1, tm, D_MODEL), lambda b, i: (b, i, 0)),
        out_shape=jax.ShapeDtypeStruct(x.shape, BF16),
        compiler_params=_params("arbitrary", "arbitrary"),
        name="modulate",
    )(*args, g, shift, scale)


def _modulate_router_kernel(x_ref, g_ref, sh_ref, sc_ref, wr_ref, o_ref, p_ref):
    xm = _rms_modulate(x_ref[0], g_ref[...], sh_ref[0], sc_ref[0])
    xh, xl = _split_bf16(xm)
    wh, wl = _split_bf16(wr_ref[...])
    logits = _dot(xh, wh) + _dot(xl, wh) + _dot(xh, wl)
    lane = lax.broadcasted_iota(jnp.int32, logits.shape, 1)
    logits = jnp.where(lane < N_EXPERTS, logits, -jnp.inf)
    e = jnp.exp(logits - jnp.max(logits, axis=-1, keepdims=True))
    p_ref[0] = e / jnp.sum(e, axis=-1, keepdims=True)
    o_ref[0] = xh


def modulate_router(x, g, shift, scale, w_router_padded):
    nb, t, _ = x.shape
    tm = min(t, 512)
    return pl.pallas_call(
        _modulate_router_kernel,
        grid=(nb, t // tm),
        in_specs=[
            pl.BlockSpec((1, tm, D_MODEL), lambda b, i: (b, i, 0)),
            pl.BlockSpec((1, D_MODEL), lambda b, i: (0, 0)),
            _row_vec_spec(shift, nb),
            _row_vec_spec(scale, nb),
            pl.BlockSpec((D_MODEL, LANES), lambda b, i: (0, 0)),
        ],
        out_specs=[
            pl.BlockSpec((1, tm, D_MODEL), lambda b, i: (b, i, 0)),
            pl.BlockSpec((1, tm, LANES), lambda b, i: (b, i, 0)),
        ],
        out_shape=[
            jax.ShapeDtypeStruct(x.shape, BF16),
            jax.ShapeDtypeStruct((nb, t, LANES), F32),
        ],
        compiler_params=_params("arbitrary", "arbitrary"),
        name="modulate_router",
    )(x, g, shift, scale, w_router_padded)


def _matmul_kernel(*refs, n_a, n_extra, n_out, epilogue, row_scaled):
    a_refs = refs[:n_a]
    w_ref = refs[n_a]
    n_w = 2 if row_scaled else 1
    extra = refs[n_a + n_w:n_a + n_w + n_extra]
    outs = refs[n_a + n_w + n_extra:n_a + n_w + n_extra + n_out]
    wbf_ref = refs[-1]

    @pl.when(pl.program_id(1) == 0)
    def _():
        if row_scaled:
            wbf_ref[...] = (w_ref[...] * refs[n_a + 1][...]).astype(BF16)
        else:
            wbf_ref[...] = w_ref[...].astype(BF16)

    acc = None
    off = 0
    for a_ref in a_refs:
        k = a_ref.shape[-1]
        part = _dot(a_ref[...], wbf_ref[off:off + k, :])
        acc = part if acc is None else acc + part
        off += k
    epilogue(acc, extra, outs, a_refs)


def fused_matmul(a_list, w, layer, col0, ncols, tm, tn, epilogue, extras, extra_specs, out_dtypes, name,
                 w_row_scale=None):
    m = a_list[0].shape[0]
    k_total = w.shape[1]
    assert sum(a.shape[1] for a in a_list) == k_total
    assert m % tm == 0 and ncols % tn == 0 and col0 % tn == 0
    jb = col0 // tn
    in_specs = [pl.BlockSpec((tm, a.shape[1]), lambda j, i: (i, 0)) for a in a_list]
    in_specs.append(pl.BlockSpec((None, k_total, tn), lambda j, i: (layer, 0, jb + j)))
    weights = [w]
    if w_row_scale is not None:
        in_specs.append(pl.BlockSpec((None, k_total, 1), lambda j, i: (layer, 0, 0)))
        weights.append(w_row_scale)
    in_specs.extend(extra_specs)
    out_specs = [pl.BlockSpec((tm, tn), lambda j, i: (i, j)) for _ in out_dtypes]
    out_shape = [jax.ShapeDtypeStruct((m, ncols), dt) for dt in out_dtypes]
    kern = functools.partial(_matmul_kernel, n_a=len(a_list), n_extra=len(extras), n_out=len(out_dtypes),
                             epilogue=epilogue, row_scaled=w_row_scale is not None)
    return pl.pallas_call(
        kern,
        grid=(ncols // tn, m // tm),
        in_specs=in_specs,
        out_specs=out_specs,
        out_shape=out_shape,
        scratch_shapes=[pltpu.VMEM((k_total, tn), BF16)],
        compiler_params=_params("arbitrary", "arbitrary"),
        name=name,
    )(*a_list, *weights, *extras)


def _epi_cast(acc, extra, outs, a_refs):
    outs[0][...] = acc.astype(outs[0].dtype)


def _epi_gelu(acc, extra, outs, a_refs):
    outs[0][...] = (0.5 * acc * (1.0 + lax.erf(acc * (1.0 / math.sqrt(2.0))))).astype(outs[0].dtype)


def _epi_dft(acc, extra, outs, a_refs):
    ch_ref, cl_ref, sh_ref, sl_ref = extra
    a_ref, b_ref = outs
    zh, zl = _split_bf16(acc)
    for g in range(acc.shape[1] // GROUP_DIM):
        sl = slice(g * GROUP_DIM, (g + 1) * GROUP_DIM)
        a_ref[:, sl] = (_dot(zh[:, sl], ch_ref[...]) + _dot(zl[:, sl], ch_ref[...])
                        + _dot(zh[:, sl], cl_ref[...])).astype(a_ref.dtype)
        b_ref[:, sl] = (_dot(zh[:, sl], sh_ref[...]) + _dot(zl[:, sl], sh_ref[...])
                        + _dot(zh[:, sl], sl_ref[...])).astype(b_ref.dtype)


def _epi_residual(acc, extra, outs, a_refs):
    h_ref, gate_ref = extra
    outs[0][...] = h_ref[...] + gate_ref[0] * acc


def _epi_residual_pos(acc, extra, outs, a_refs):
    x_ref, pos_ref, gate_ref = extra
    outs[0][...] = x_ref[...] + pos_ref[...] + gate_ref[0] * acc


def _epi_residual_rownorm(acc, extra, outs, a_refs):
    h_ref, gate_ref = extra
    a = a_refs[0][...].astype(F32)
    inv = lax.rsqrt(jnp.mean(a * a, axis=1, keepdims=True) + EPS)
    outs[0][...] = h_ref[...] + gate_ref[0] * (acc * inv)


def _gate_spec(gate, t, tm, tn):
    per_batch = t // tm
    if gate.shape[0] == 1:
        return pl.BlockSpec((1, 1, tn), lambda j, i: (0, 0, j))
    return pl.BlockSpec((1, 1, tn), lambda j, i: (i // per_batch, 0, j))


def _residual_specs(h_flat, gate, t, tm, tn):
    return [pl.BlockSpec((tm, tn), lambda j, i: (i, j)), _gate_spec(gate, t, tm, tn)]


def _gmlp_kernel(u_ref, v_ref, vg_ref, ws_ref, bs_ref, o_ref):
    for c in range(v_ref.shape[1] // CHUNK):
        rows = slice(c * CHUNK, (c + 1) * CHUNK)
        v = v_ref[0, rows, :].astype(F32)
        vn = (v * lax.rsqrt(jnp.mean(v * v, axis=-1, keepdims=True) + EPS) * vg_ref[...]).astype(BF16)
        for g in range(A_GROUPS):
            sl = slice(g * GROUP_DIM, (g + 1) * GROUP_DIM)
            s = _dot(ws_ref[g].astype(BF16), vn[:, sl]) + bs_ref[:, g:g + 1]
            o_ref[0, rows, sl] = (u_ref[0, rows, sl].astype(F32) * s).astype(o_ref.dtype)


def chunk_gmlp(a, v_g, w_s, b_s_t, layer):
    nb, t, _ = a.shape
    rows = min(t, 512)
    return pl.pallas_call(
        _gmlp_kernel,
        grid=(nb, t // rows),
        in_specs=[
            pl.BlockSpec((1, rows, A_WIDTH), lambda b, c: (b, c, 0)),
            pl.BlockSpec((1, rows, A_WIDTH), lambda b, c: (b, c, 1)),
            pl.BlockSpec((None, 1, A_WIDTH), lambda b, c: (layer, 0, 0)),
            pl.BlockSpec((None, A_GROUPS, CHUNK, CHUNK), lambda b, c: (layer, 0, 0, 0)),
            pl.BlockSpec((None, CHUNK, A_GROUPS), lambda b, c: (layer, 0, 0)),
        ],
        out_specs=pl.BlockSpec((1, rows, A_WIDTH), lambda b, c: (b, c, 0)),
        out_shape=jax.ShapeDtypeStruct((nb, t, A_WIDTH), BF16),
        compiler_params=_params("arbitrary", "arbitrary"),
        name="chunk_gmlp",
    )(a, a, v_g.reshape(v_g.shape[0], 1, A_WIDTH), w_s, b_s_t)


def _dft_tables(n):
    k = np.arange(n, dtype=np.int64)
    ang = 2.0 * np.pi * ((k[:, None] * k[None, :]) % n).astype(np.float64) / n
    return np.cos(ang), np.sin(ang)


def _hi_lo(table):
    hi = jnp.asarray(table, dtype=F32).astype(BF16)
    lo = (jnp.asarray(table, dtype=F32) - hi.astype(F32)).astype(BF16)
    return hi, lo


def _token_dft_kernel(ct_ref, st_ref, a_ref, b_ref, o_ref, *, scale):
    y = _dot(ct_ref[...], a_ref[0]) - _dot(st_ref[...], b_ref[0])
    o_ref[0] = (y * scale).astype(o_ref.dtype)


def token_dft(a, b):
    nb, t, w = a.shape
    cos_t, sin_t = _dft_tables(t)
    ct = jnp.asarray(cos_t, dtype=F32).astype(BF16)
    st = jnp.asarray(sin_t, dtype=F32).astype(BF16)
    tm = min(t, 512)
    kern = functools.partial(_token_dft_kernel, scale=1.0 / math.sqrt(t * GROUP_DIM))
    return pl.pallas_call(
        kern,
        grid=(nb, t // tm),
        in_specs=[
            pl.BlockSpec((tm, t), lambda b_, i: (i, 0)),
            pl.BlockSpec((tm, t), lambda b_, i: (i, 0)),
            pl.BlockSpec((1, t, w), lambda b_, i: (b_, 0, 0)),
            pl.BlockSpec((1, t, w), lambda b_, i: (b_, 0, 0)),
        ],
        out_specs=pl.BlockSpec((1, tm, w), lambda b_, i: (b_, i, 0)),
        out_shape=jax.ShapeDtypeStruct((nb, t, w), BF16),
        compiler_params=_params("arbitrary", "arbitrary"),
        name="token_dft",
    )(ct, st, a, b)


def mixer_ab(xn, h, gate, ab_w_in, ab_w_out, gm_v_g, gm_w_s, gm_b_s_t, layer, pos=None):
    nb, t, _ = xn.shape
    m = nb * t
    xf = xn.reshape(m, D_MODEL)
    tm = min(m, 512)
    (a,) = fused_matmul([xf], ab_w_in, layer, 0, 2 * A_WIDTH, tm, 1024, _epi_gelu, [], [], [BF16], "ab_in_gelu")
    c128, s128 = _dft_tables(GROUP_DIM)
    tables = [*_hi_lo(c128), *_hi_lo(s128)]
    table_specs = [pl.BlockSpec((GROUP_DIM, GROUP_DIM), lambda j, i: (0, 0)) for _ in tables]
    za, zb = fused_matmul([xf], ab_w_in, layer, 2 * A_WIDTH, B_WIDTH, tm, 1024, _epi_dft, tables, table_specs,
                          [BF16, BF16], "ab_in_dft")
    ya = chunk_gmlp(a.reshape(nb, t, 2 * A_WIDTH), gm_v_g, gm_w_s, gm_b_s_t, layer)
    yb = token_dft(za.reshape(nb, t, B_WIDTH), zb.reshape(nb, t, B_WIDTH))
    hf = h.reshape(m, D_MODEL)
    tn = 1024
    if pos is None:
        epi, extras, specs = _epi_residual, [hf, gate], _residual_specs(hf, gate, t, tm, tn)
    else:
        per_batch = t // tm
        epi, extras = _epi_residual_pos, [hf, pos, gate]
        specs = [pl.BlockSpec((tm, tn), lambda j, i: (i, j)), pl.BlockSpec((tm, tn), lambda j, i: (i % per_batch, j)),
                 _gate_spec(gate, t, tm, tn)]
    (out,) = fused_matmul([ya.reshape(m, A_WIDTH), yb.reshape(m, B_WIDTH)], ab_w_out, layer, 0, D_MODEL, tm, tn,
                          epi, extras, specs, [F32], "ab_out")
    return out.reshape(nb, t, D_MODEL)


def _conv_kernel(prev_ref, main_ref, next_ref, w_ref, b_ref, o_ref, *, rows):
    r = pl.program_id(1)
    last = pl.num_programs(1) - 1
    prev = jnp.where(r > 0, prev_ref[0].astype(F32), 0.0)
    nxt = jnp.where(r < last, next_ref[0].astype(F32), 0.0)
    full = jnp.concatenate([prev, main_ref[0].astype(F32), nxt], axis=0)
    n = full.shape[0]
    w = w_ref[...]
    acc = None
    for k in range(D_CONV):
        shift = (D_CONV // 2 - k) % n
        term = (full if shift == 0 else pltpu.roll(full, shift, 0)) * w[k:k + 1, :]
        acc = term if acc is None else acc + term
    y = acc[BF16_SUBLANES:BF16_SUBLANES + rows, :] + b_ref[...]
    o_ref[0] = _silu(y).astype(o_ref.dtype)


def dwconv_silu(z, conv_w, conv_b, layer):
    nb, t, ch = z.shape
    rows = min(t, 512)
    tc = 1024
    halo = BF16_SUBLANES
    per = rows // halo
    n_halo = t // halo
    kern = functools.partial(_conv_kernel, rows=rows)
    return pl.pallas_call(
        kern,
        grid=(nb, t // rows, ch // tc),
        in_specs=[
            pl.BlockSpec((1, halo, tc), lambda b, r, c: (b, jnp.maximum(r * per - 1, 0), c)),
            pl.BlockSpec((1, rows, tc), lambda b, r, c: (b, r, c)),
            pl.BlockSpec((1, halo, tc), lambda b, r, c: (b, jnp.minimum((r + 1) * per, n_halo - 1), c)),
            pl.BlockSpec((None, D_CONV, tc), lambda b, r, c: (layer, 0, c)),
            pl.BlockSpec((None, 1, tc), lambda b, r, c: (layer, 0, c)),
        ],
        out_specs=pl.BlockSpec((1, rows, tc), lambda b, r, c: (b, r, c)),
        out_shape=jax.ShapeDtypeStruct(z.shape, BF16),
        compiler_params=_params("arbitrary", "arbitrary", "arbitrary"),
        name="dwconv_silu",
    )(z, z, z, conv_w, conv_b.reshape(conv_b.shape[0], 1, conv_b.shape[1]))


def _softplus(x):
    return jnp.maximum(x, 0.0) + jnp.log(1.0 + jnp.exp(-jnp.abs(x)))


def _prefix_sum(x, axis):
    n = x.shape[axis]
    idx = lax.broadcasted_iota(jnp.int32, x.shape, axis)
    s = 1
    while s < n:
        x = x + jnp.where(idx >= s, pltpu.roll(x, s, axis), 0.0)
        s *= 2
    return x


def _ssd_prep_kernel(raw_ref, bias_ref, a_ref, cumc_ref, cumr_ref, dtr_ref, wr_ref):
    L = SSD_CHUNK
    gw = 2 * HEADS_PER_GROUP
    lane = lax.broadcasted_iota(jnp.int32, (L, raw_ref.shape[2]), 1)
    is_bwd = (lane % gw) >= HEADS_PER_GROUP

    def chunk(c, carry):
        rows = pl.ds(pl.multiple_of(c * L, L), L)
        dt = _softplus(raw_ref[0, rows, :] + bias_ref[...])
        la = dt * (a_ref[...] * LOG2E)
        cum = _prefix_sum(la, 0)
        tot = cum[L - 1:L, :]
        cum = jnp.where(is_bwd, tot - cum + la, cum)
        w = jnp.exp2(tot - cum) * dt
        cum_t = cum.T
        dt_t = dt.T
        w_t = w.T
        for g in range(N_BC_GROUPS):
            ls = slice(g * gw, (g + 1) * gw)
            cumc_ref[0, g, rows, :] = cum[:, ls]
            cumr_ref[0, g, :, rows] = cum_t[ls, :]
            dtr_ref[0, g, :, rows] = dt_t[ls, :]
            wr_ref[0, g, :, rows] = w_t[ls, :]
        return carry

    lax.fori_loop(0, raw_ref.shape[1] // L, chunk, 0)


def ssd_prep(dt_raw, bias, a):
    nb, t, n = dt_raw.shape
    g, gw = N_BC_GROUPS, 2 * HEADS_PER_GROUP
    col = jax.ShapeDtypeStruct((nb, g, t, gw), F32)
    row = jax.ShapeDtypeStruct((nb, g, gw, t), F32)
    col_spec = pl.BlockSpec((1, g, t, gw), lambda b: (b, 0, 0, 0))
    row_spec = pl.BlockSpec((1, g, gw, t), lambda b: (b, 0, 0, 0))
    return pl.pallas_call(
        _ssd_prep_kernel,
        grid=(nb,),
        in_specs=[
            pl.BlockSpec((1, t, n), lambda b: (b, 0, 0)),
            pl.BlockSpec((1, n), lambda b: (0, 0)),
            pl.BlockSpec((1, n), lambda b: (0, 0)),
        ],
        out_specs=[col_spec, row_spec, row_spec, row_spec],
        out_shape=[col, row, row, row],
        compiler_params=_params("arbitrary"),
        name="ssd_prep",
    )(dt_raw, bias, a)


def _ssd_chunk(c, direction, with_y, x_ref, b_ref, c_ref, cumc_ref, cumr_ref, dtr_ref, wr_ref, y_ref, state_ref):
    L = SSD_CHUNK
    hp = HEADS_PER_GROUP
    h0 = direction * hp
    rows = pl.ds(pl.multiple_of(c * L, L), L)
    lane_lo = lax.broadcasted_iota(jnp.int32, (L, 2 * HEADDIM), 1) < HEADDIM
    row_lo = lane_lo[:1]
    edge = L - 1 if direction == 0 else 0

    bm = b_ref[0, rows, :]
    bm_t = bm.astype(F32).T
    cumr = cumr_ref[0, 0, :, rows]
    wr = wr_ref[0, 0, :, rows]

    if with_y:
        cm = c_ref[0, rows, :]
        cb = lax.dot_general(cm, bm, (((1,), (1,)), ((), ())), preferred_element_type=F32)
        ii = lax.broadcasted_iota(jnp.int32, (L, L), 0)
        jj = lax.broadcasted_iota(jnp.int32, (L, L), 1)
        causal = (jj <= ii) if direction == 0 else (jj >= ii)
        cbm = jnp.where(causal, cb, 0.0).astype(BF16)
        cumc = cumc_ref[0, 0, rows, :]
        dtr = dtr_ref[0, 0, :, rows].astype(BF16)

    for p in range(hp // 2):
        k1, k2 = h0 + 2 * p, h0 + 2 * p + 1
        cols = slice(p * 2 * HEADDIM, (p + 1) * 2 * HEADDIM)
        s_prev = state_ref[direction, :, cols]
        xp = x_ref[0, rows, cols]
        zero = jnp.zeros_like(xp)
        rhs2 = jnp.concatenate([jnp.where(lane_lo, xp, zero), jnp.where(lane_lo, zero, xp)], axis=0)
        if with_y:
            ms, cum_b = [], []
            for k in (k1, k2):
                cum_b.append(jnp.broadcast_to(cumc[:, k:k + 1], (L, L)))
                seg = jnp.minimum(cum_b[-1] - cumr[k:k + 1, :], 0.0)
                ms.append(cbm * jnp.exp2(seg).astype(BF16) * dtr[k:k + 1, :])
            edec = jnp.exp2(jnp.where(lane_lo, cum_b[0], cum_b[1]))
            y_ref[direction, rows, cols] = (_dot(jnp.concatenate(ms, axis=1), rhs2)
                                            + _dot(cm, s_prev.astype(BF16)) * edec)
        bw = jnp.concatenate([(bm_t * wr[k1:k1 + 1, :]).astype(BF16), (bm_t * wr[k2:k2 + 1, :]).astype(BF16)], axis=1)
        d1 = jnp.exp2(cumr[k1:k1 + 1, edge:edge + 1])
        d2 = jnp.exp2(cumr[k2:k2 + 1, edge:edge + 1])
        dec = jnp.where(row_lo, jnp.broadcast_to(d1, row_lo.shape), jnp.broadcast_to(d2, row_lo.shape))
        state_ref[direction, :, cols] = s_prev * dec + _dot(bw, rhs2)


def _ssd_kernel(*refs, n_chunks, with_y):
    if with_y:
        (x_ref, b_ref, c_ref, z_ref, cumc_ref, cumr_ref, dtr_ref, wr_ref, dskip_ref, s0_ref,
         y_out_ref, sfin_ref, state_ref, y_ref) = refs
        state_ref[...] = s0_ref[:, 0, 0]
    else:
        x_ref, b_ref, cumr_ref, wr_ref, sfin_ref, state_ref = refs
        c_ref = z_ref = cumc_ref = dtr_ref = y_ref = None
        state_ref[...] = jnp.zeros_like(state_ref)
    args =(x_ref, b_ref, c_ref, cumc_ref, cumr_ref, dtr_ref, wr_ref, y_ref, state_ref)

    def both(c, carry):
        _ssd_chunk(c, 0, with_y, *args)
        _ssd_chunk(n_chunks - 1 - c, 1, with_y, *args)
        return carry

    lax.fori_loop(0, n_chunks, both, 0, unroll=min(4, n_chunks))
    sfin_ref[:, 0, 0] = state_ref[...]

    if with_y:
        def gate(c, carry):
            rows = pl.ds(pl.multiple_of(c * SSD_CHUNK, SSD_CHUNK), SSD_CHUNK)
            y = y_ref[0, rows, :] + y_ref[1, rows, :] + x_ref[0, rows, :].astype(F32) * dskip_ref[0]
            y_out_ref[0, rows, :] = (y * _silu(z_ref[0, rows, :].astype(F32))).astype(y_out_ref.dtype)
            return carry

        lax.fori_loop(0, n_chunks, gate, 0)


def ssd_scan(xbc, prep, d_skip=None, s0=None, z=None):
    nb, t, _ = xbc.shape
    g, gw = N_BC_GROUPS, 2 * HEADS_PER_GROUP
    cumc, cumr, dtr, wr = prep
    with_y = z is not None
    b_off = D_INNER // D_STATE
    c_off = (D_INNER + GN) // D_STATE
    x_spec = pl.BlockSpec((1, t, GROUP_WIDTH), lambda b, gi: (b, 0, gi))
    b_spec = pl.BlockSpec((1, t, D_STATE), lambda b, gi: (b, 0, b_off + gi))
    c_spec = pl.BlockSpec((1, t, D_STATE), lambda b, gi: (b, 0, c_off + gi))
    col_spec = pl.BlockSpec((1, 1, t, gw), lambda b, gi: (b, gi, 0, 0))
    row_spec = pl.BlockSpec((1, 1, gw, t), lambda b, gi: (b, gi, 0, 0))
    s_spec = pl.BlockSpec((2, 1, 1, D_STATE, GROUP_WIDTH), lambda b, gi: (0, b, gi, 0, 0))
    s_shape = jax.ShapeDtypeStruct((2, nb, g, D_STATE, GROUP_WIDTH), F32)
    state = pltpu.VMEM((2, D_STATE, GROUP_WIDTH), F32)
    kern = functools.partial(_ssd_kernel, n_chunks=t // SSD_CHUNK, with_y=with_y)
    if not with_y:
        assert s0 is None and d_skip is None
        return pl.pallas_call(
            kern,
            grid=(nb, g),
            in_specs=[x_spec, b_spec, row_spec, row_spec],
            out_specs=s_spec,
            out_shape=s_shape,
            scratch_shapes=[state],
            compiler_params=_params("arbitrary", "arbitrary"),
            name="ssd_states",
        )(xbc, xbc, cumr, wr)
    dskip = jnp.repeat(d_skip, HEADDIM).reshape(g, 1, GROUP_WIDTH)
    return pl.pallas_call(
        kern,
        grid=(nb, g),
        in_specs=[x_spec, b_spec, c_spec, x_spec, col_spec, row_spec, row_spec, row_spec,
                  pl.BlockSpec((1, 1, GROUP_WIDTH), lambda b, gi: (gi, 0, 0)), s_spec],
        out_specs=[x_spec, s_spec],
        out_shape=[jax.ShapeDtypeStruct((nb, t, D_INNER), BF16), s_shape],
        scratch_shapes=[state, pltpu.VMEM((2, t, GROUP_WIDTH), F32)],
        compiler_params=_params("arbitrary", "arbitrary"),
        name="ssd_scan",
    )(xbc, xbc, xbc, z, cumc, cumr, dtr, wr, dskip, s0)


def _group_major(v):
    return v.reshape(2, N_BC_GROUPS, HEADS_PER_GROUP).transpose(1, 0, 2).reshape(1, 2 * N_HEADS)


def ssd_project(xn, w_in, w_dt, conv_w, conv_b, dt_bias, a, layer, with_z):
    nb, t, _ = xn.shape
    m = nb * t
    xf = xn.reshape(m, D_MODEL)
    tm = min(m, 512)
    conv_cols = CONV_DIM if with_z else D_INNER + GN
    (xbc,) = fused_matmul([xf], w_in, layer, D_INNER, conv_cols, tm, 1024, _epi_cast, [], [], [BF16], "ssd_in_xbc")
    (dt_raw,) = fused_matmul([xf], w_dt, 0, 0, 2 * N_HEADS, tm, 2 * N_HEADS, _epi_cast, [], [], [F32], "ssd_in_dt")
    xbc = dwconv_silu(xbc.reshape(nb, t, conv_cols), conv_w, conv_b, layer)
    prep = ssd_prep(dt_raw.reshape(nb, t, 2 * N_HEADS), _group_major(dt_bias), _group_major(a))
    z = None
    if with_z:
        (z,) = fused_matmul([xf], w_in, layer, 0, D_INNER, tm, 1024, _epi_cast, [], [], [BF16], "ssd_in_z")
        z = z.reshape(nb, t, D_INNER)
    return xbc, prep, z


def ssd_mix(xn, xc, h, gate, w_in, conv_w, conv_b, dt_bias, a_log, d_skip, norm_g, w_out, layer):
    nb, t, _ = xn.shape
    m = nb * t
    a = -jnp.exp(a_log[layer].astype(F32))
    w_dt = w_in[layer, :, D_INNER + CONV_DIM:].reshape(D_MODEL, 2, N_BC_GROUPS, HEADS_PER_GROUP)
    w_dt = w_dt.transpose(0, 2, 1, 3).reshape(1, D_MODEL, 2 * N_HEADS)
    xbc_c, prep_c, _ = ssd_project(xc, w_in, w_dt, conv_w, conv_b, dt_bias[layer], a, layer, False)
    s_ctx = ssd_scan(xbc_c, prep_c)
    xbc, prep, z = ssd_project(xn, w_in, w_dt, conv_w, conv_b, dt_bias[layer], a, layer, True)
    y, _ = ssd_scan(xbc, prep, d_skip[layer], s_ctx, z)
    hf = h.reshape(m, D_MODEL)
    tm, tn = 512, 512
    (out,) = fused_matmul([y.reshape(m, D_INNER)], w_out, layer, 0, D_MODEL, tm, tn, _epi_residual_rownorm,
                          [hf, gate], _residual_specs(hf, gate, t, tm, tn), [F32], "ssd_out",
                          w_row_scale=norm_g.reshape(norm_g.shape[0], D_INNER, 1))
    return out.reshape(nb, t, D_MODEL)


def _route_kernel(p_ref, tri_ref, pos_ref, post_ref, gate_ref, *, cap):
    probs = p_ref[0]
    t = probs.shape[0]
    pe = probs.T[:N_EXPERTS, :]
    bits = pltpu.bitcast(pe, jnp.int32)

    def step(i, v):
        cand = v | jnp.left_shift(jnp.int32(1), 30 - i)
        cnt = jnp.sum((bits >= cand).astype(F32), axis=1, keepdims=True)
        return jnp.where(cnt >= cap, cand, v)

    thr = lax.fori_loop(0, 31, step, jnp.zeros((N_EXPERTS, 1), jnp.int32))
    gt = bits > thr
    eq = bits == thr
    need = cap - jnp.sum(gt.astype(F32), axis=1, keepdims=True)
    tri = tri_ref[...]
    eq_rank = _dot(eq.astype(BF16), tri)
    sel = gt | (eq & (eq_rank <= need))
    pos = _dot(sel.astype(BF16), tri) - 1.0
    pos = jnp.where(sel, pos, -1.0)
    pos_ref[0] = pos
    pos_full = jnp.concatenate([pos, jnp.full((LANES - N_EXPERTS, t), -1.0, F32)], axis=0)
    pos_t = pos_full.T
    post_ref[0] = pos_t
    gate = jnp.where(pos_t >= 0.0, probs, 0.0)
    hi = gate.astype(BF16).astype(F32)
    mid = (gate - hi).astype(BF16).astype(F32)
    lo = (gate - hi - mid).astype(BF16).astype(F32)
    gate_ref[0] = (hi + pltpu.roll(mid, N_EXPERTS, 1) + pltpu.roll(lo, 2 * N_EXPERTS, 1)).astype(BF16)


def route(probs, cap):
    nb, t, _ = probs.shape
    idx = np.arange(t)
    tri = jnp.asarray((idx[:, None] <= idx[None, :]).astype(np.float32), dtype=BF16)
    kern = functools.partial(_route_kernel, cap=cap)
    return pl.pallas_call(
        kern,
        grid=(nb,),
        in_specs=[
            pl.BlockSpec((1, t, LANES), lambda b: (b, 0, 0)),
            pl.BlockSpec((t, t), lambda b: (0, 0)),
        ],
        out_specs=[
            pl.BlockSpec((1, N_EXPERTS, t), lambda b: (b, 0, 0)),
            pl.BlockSpec((1, t, LANES), lambda b: (b, 0, 0)),
            pl.BlockSpec((1, t, LANES), lambda b: (b, 0, 0)),
        ],
        out_shape=[
            jax.ShapeDtypeStruct((nb, N_EXPERTS, t), F32),
            jax.ShapeDtypeStruct((nb, t, LANES), F32),
            jax.ShapeDtypeStruct((nb, t, LANES), BF16),
        ],
        compiler_params=_params("arbitrary"),
        name="route",
    )(probs, tri)


def _gather_kernel(pos_ref, x_ref, gate_ref, *rest, cap, n_batch):
    o_ref, gs_ref = rest[-2:]
    b = pl.program_id(0)
    e = pl.program_id(1)

    @pl.when(b < n_batch)
    def _():
        pos = pos_ref[0, 0]
        t = pos.shape[1]
        slot = lax.broadcasted_iota(jnp.int32, (cap, t), 0).astype(F32)
        onehot = (slot == pos).astype(BF16)
        o_ref[0] = _dot(onehot, x_ref[0]).astype(o_ref.dtype)
        pieces = _dot(onehot, gate_ref[0])
        lane = lax.broadcasted_iota(jnp.int32, pieces.shape, 1)
        mine = (lane < 3 * N_EXPERTS) & (lane % N_EXPERTS == e)
        gs_ref[0] = jnp.sum(jnp.where(mine, pieces, 0.0), axis=1, keepdims=True)

    @pl.when(b >= n_batch)
    def _():
        o_ref[...] = jnp.zeros_like(o_ref)
        gs_ref[...] = jnp.zeros_like(gs_ref)


def gather_tokens(pos, xm, gate_split, cap, total_rows=None, row0=0, dest=None):
    nb, t, _ = xm.shape
    total_rows = nb * cap if total_rows is None else total_rows
    assert row0 % cap == 0
    blk0 = row0 // cap
    tail = 0 if dest is not None else total_rows - row0 - nb * cap
    assert tail % cap == 0
    kern = functools.partial(_gather_kernel, cap=cap, n_batch=nb)
    last = nb - 1
    in_specs = [
        pl.BlockSpec((1, 1, 1, t), lambda b, e: (jnp.minimum(b, last), e, 0, 0)),
        pl.BlockSpec((1, t, D_MODEL), lambda b, e: (jnp.minimum(b, last), 0, 0)),
        pl.BlockSpec((1, t, LANES), lambda b, e: (jnp.minimum(b, last), 0, 0)),
    ]
    args = [pos.reshape(nb, N_EXPERTS, 1, t), xm, gate_split]
    aliases = {}
    if dest is not None:
        in_specs += [pl.BlockSpec(memory_space=pl.ANY), pl.BlockSpec(memory_space=pl.ANY)]
        args += list(dest)
        aliases = {3: 0, 4: 1}
    return pl.pallas_call(
        kern,
        grid=(nb + tail // cap, N_EXPERTS),
        in_specs=in_specs,
        out_specs=[
            pl.BlockSpec((1, cap, D_MODEL), lambda b, e: (e, blk0 + b, 0)),
            pl.BlockSpec((1, cap, 1), lambda b, e: (e, blk0 + b, 0)),
        ],
        out_shape=[
            jax.ShapeDtypeStruct((N_EXPERTS, total_rows, D_MODEL), BF16),
            jax.ShapeDtypeStruct((N_EXPERTS, total_rows, 1), F32),
        ],
        input_output_aliases=aliases,
        compiler_params=_params("arbitrary", "arbitrary"),
        name="gather_tokens",
    )(*args)


def _ffn_kernel(x_ref, gs_ref, wg_ref, wu_ref, wd_ref, o_ref, acc_ref):
    f = pl.program_id(2)
    x = x_ref[0]
    gate = _dot(x, wg_ref[...].astype(BF16))
    up = _dot(x, wu_ref[...].astype(BF16))
    hid = (_silu(gate) * up).astype(BF16)

    @pl.when(f == 0)
    def _():
        acc_ref[...] = jnp.zeros_like(acc_ref)

    acc_ref[...] += _dot(hid, wd_ref[...].astype(BF16))

    @pl.when(f == pl.num_programs(2) - 1)
    def _():
        o_ref[0] = (acc_ref[...] * gs_ref[0]).astype(o_ref.dtype)


def expert_ffn(xe, gate_slot, w_gate, w_up, w_down, layer):
    _, m, _ = xe.shape
    tm = m if m <= 1024 else m // 2
    assert m % tm == 0 and tm % BF16_SUBLANES == 0
    tf = 256
    return pl.pallas_call(
        _ffn_kernel,
        grid=(N_EXPERTS, m // tm, D_MODEL // tf),
        in_specs=[
            pl.BlockSpec((1, tm, D_MODEL), lambda e, i, f: (e, i, 0)),
            pl.BlockSpec((1, tm, 1), lambda e, i, f: (e, i, 0)),
            pl.BlockSpec((None, None, D_MODEL, tf), lambda e, i, f: (layer, e, 0, f)),
            pl.BlockSpec((None, None, D_MODEL, tf), lambda e, i, f: (layer, e, 0, f)),
            pl.BlockSpec((None, None, tf, D_MODEL), lambda e, i, f: (layer, e, f, 0)),
        ],
        out_specs=pl.BlockSpec((1, tm, D_MODEL), lambda e, i, f: (e, i, 0)),
        out_shape=jax.ShapeDtypeStruct(xe.shape, BF16),
        scratch_shapes=[pltpu.VMEM((tm, D_MODEL), F32)],
        compiler_params=_params("arbitrary", "arbitrary", "arbitrary"),
        name="expert_ffn",
    )(xe, gate_slot, w_gate, w_up, w_down)


def _combine_kernel(*refs, cap, final_norm):
    post_ref, y_ref, h_ref, g2_ref = refs[:4]
    o_ref = refs[-1]
    tq = post_ref.shape[1]
    slot = lax.broadcasted_iota(jnp.int32, (tq, cap), 1).astype(F32)
    pos_t = post_ref[0]
    acc = None
    for e in range(N_EXPERTS):
        onehot = (slot == pos_t[:, e:e + 1]).astype(BF16)
        part = _dot(onehot, y_ref[e])
        acc = part if acc is None else acc + part
    out = h_ref[0] + g2_ref[0] * acc
    if final_norm:
        out = out * lax.rsqrt(jnp.mean(out * out, axis=-1, keepdims=True) + EPS) * refs[4][...]
    o_ref[0] = out


def combine(pos_t, ye, h, g2, cap, final_g=None, row0=0):
    nb, t, _ = h.shape
    tq = min(t, 256)
    assert row0 % cap == 0
    blk0 = row0 // cap
    if g2.shape[0] == 1:
        g2_spec = pl.BlockSpec((1, 1, D_MODEL), lambda b, i: (0, 0, 0))
    else:
        g2_spec = pl.BlockSpec((1, 1, D_MODEL), lambda b, i: (b, 0, 0))
    in_specs = [
        pl.BlockSpec((1, tq, LANES), lambda b, i: (b, i, 0)),
        pl.BlockSpec((N_EXPERTS, cap, D_MODEL), lambda b, i: (0, blk0 + b, 0)),
        pl.BlockSpec((1, tq, D_MODEL), lambda b, i: (b, i, 0)),
        g2_spec,
    ]
    args = [pos_t, ye, h, g2]
    if final_g is not None:
        in_specs.append(pl.BlockSpec((1, D_MODEL), lambda b, i: (0, 0)))
        args.append(final_g)
    kern = functools.partial(_combine_kernel, cap=cap, final_norm=final_g is not None)
    return pl.pallas_call(
        kern,
        grid=(nb, t // tq),
        in_specs=in_specs,
        out_specs=pl.BlockSpec((1, tq, D_MODEL), lambda b, i: (b, i, 0)),
        out_shape=jax.ShapeDtypeStruct(h.shape, F32),
        compiler_params=_params("arbitrary", "arbitrary"),
        name="combine",
    )(*args)


def expert_choice_ffn(streams, g, w_router, w_gate, w_up, w_down, layer, final_g=None):
    wr = jnp.pad(w_router[layer], ((0, 0), (0, LANES - N_EXPERTS)))
    caps = [CAPACITY_FACTOR * h.shape[1] // N_EXPERTS for h, _, _, _ in streams]
    rows = [h.shape[0] * cap for (h, _, _, _), cap in zip(streams, caps)]
    row0 = [sum(rows[:k]) for k in range(len(rows))]
    routed, dest = [], None
    for (h, shift, scale, _), cap, r0 in zip(streams, caps, row0):
        xm, probs = modulate_router(h, g, shift, scale, wr)
        pos, pos_t, gate_split = route(probs, cap)
        dest = gather_tokens(pos, xm, gate_split, cap, sum(rows), r0, dest)
        routed.append(pos_t)
    ye = expert_ffn(*dest, w_gate, w_up, w_down, layer)
    return [combine(pos_t, ye, h, g2, cap, final_g if k == 0 else None, r0)
            for k, ((h, _, _, g2), pos_t, cap, r0) in enumerate(zip(streams, routed, caps, row0))]


def _sincos_2d(rows, cols, dim):
    quarter = dim // 4
    omega = 1.0 / (10000.0 ** (np.arange(quarter, dtype=np.float64) / quarter))
    r = np.repeat(np.arange(rows, dtype=np.float64), cols)[:, None] * omega
    cl = np.tile(np.arange(cols, dtype=np.float64), rows)[:, None] * omega
    return jnp.asarray(np.concatenate([np.sin(r), np.cos(r), np.sin(cl), np.cos(cl)], axis=-1), dtype=F32)


def kernel(x, c, ctx, c_ctx, mod_w, mod_b, norm_g, final_g, ab_w_in, ab_w_out, gm_v_g, gm_w_s, gm_b_s,
           ssd_w_in, ssd_conv_w, ssd_conv_b, ssd_dt_bias, ssd_a_log, ssd_d, ssd_norm_g, ssd_w_out,
           moe_w_router, moe_w_gate, moe_w_up, moe_w_down):
    nb, n, _ = x.shape
    depth = mod_w.shape[0]
    pos = _sincos_2d(n // GRID_W, GRID_W, D_MODEL)
    h = x
    hc = ctx
    cond = jnp.concatenate([c, c_ctx[None], jnp.zeros((16 - nb - 1, D_MODEL), F32)], axis=0)
    gm_b_s_t = jnp.swapaxes(gm_b_s, 1, 2)
    for i in range(depth):
        last = i == depth - 1
        even = i % 2 == 0
        j = i // 2
        ctx_reaches_latent = (not last) or (not even)
        mod = adaln(cond, mod_w, mod_b, i).reshape(16, 6, 1, D_MODEL)
        sh1, sc1, g1, sh2, sc2, g2 = (mod[:nb, k] for k in range(6))
        csh1, csc1, cg1, csh2, csc2, cg2 = (mod[nb:nb + 1, k] for k in range(6))
        g_mix = norm_g[i, 0][None]
        g_ffn = norm_g[i, 1][None]
        if pos is not None and not even:
            h, pos = h + pos[None], None
        xn = modulate(h, g_mix, sh1, sc1, pos)
        if ctx_reaches_latent:
            xc = modulate(hc, g_mix, csh1, csc1)
        if even:
            h = mixer_ab(xn, h, g1, ab_w_in, ab_w_out, gm_v_g, gm_w_s, gm_b_s_t, j, pos)
            pos = None
            if not last:
                hc = mixer_ab(xc, hc, cg1, ab_w_in, ab_w_out, gm_v_g, gm_w_s, gm_b_s_t, j)
        else:
            assert last, "context outputs of a state-space layer are only needed before the last layer"
            h = ssd_mix(xn, xc, h, g1, ssd_w_in, ssd_conv_w, ssd_conv_b, ssd_dt_bias, ssd_a_log, ssd_d,
                        ssd_norm_g, ssd_w_out, j)
        streams = [(h, sh2, sc2, g2)]
        if not last:
            streams.append((hc, csh2, csc2, cg2))
        outs = expert_choice_ffn(streams, g_ffn, moe_w_router, moe_w_gate, moe_w_up, moe_w_down, i,
                                 final_g[None] if last else None)
        h = outs[0]
        if not last:
            hc = outs[1]
    return h
```

```python
import functools
import math

import jax
import jax.numpy as jnp
import numpy as np
from jax import lax
from jax.experimental import pallas as pl
from jax.experimental.pallas import tpu as pltpu

F32 = jnp.float32
BF16 = jnp.bfloat16

D_MODEL = 2048
GRID_W = 64
EPS = 1e-6

A_WIDTH = D_MODEL // 2
GROUP_DIM = 128
A_GROUPS = A_WIDTH // GROUP_DIM
CHUNK = 128
B_WIDTH = D_MODEL // 2

D_INNER = 2 * D_MODEL
HEADDIM = 64
N_HEADS = D_INNER // HEADDIM
D_STATE = 128
N_BC_GROUPS = 8
HEADS_PER_GROUP = N_HEADS // N_BC_GROUPS
D_CONV = 5
SSD_CHUNK = 128
GN = N_BC_GROUPS * D_STATE
CONV_DIM = D_INNER + 2 * GN
GROUP_WIDTH = HEADS_PER_GROUP * HEADDIM

N_EXPERTS = 16
CAPACITY_FACTOR = 2

LANES = 128
BF16_SUBLANES = 16
VMEM_LIMIT = 56 * 1024 * 1024
LOG2E = 1.4426950408889634


def _params(*sem):
    return pltpu.CompilerParams(dimension_semantics=sem, vmem_limit_bytes=VMEM_LIMIT)


def _silu(x):
    return x / (1.0 + jnp.exp2(x * -LOG2E))


def _split_bf16(x):
    hi = x.astype(BF16)
    lo = (x - hi.astype(F32)).astype(BF16)
    return hi, lo


def _dot(a, b):
    return jnp.dot(a, b, preferred_element_type=F32)


def _adaln_kernel(c_ref, w_ref, b_ref, o_ref):
    a = _silu(c_ref[...]).astype(BF16)
    o_ref[...] = _dot(a, w_ref[...].astype(BF16)) + b_ref[...]


def adaln(cond, mod_w, mod_b, layer):
    rows = cond.shape[0]
    tn = 1024
    n = 6 * D_MODEL
    return pl.pallas_call(
        _adaln_kernel,
        grid=(n // tn,),
        in_specs=[
            pl.BlockSpec((rows, D_MODEL), lambda j: (0, 0)),
            pl.BlockSpec((None, D_MODEL, tn), lambda j: (layer, 0, j)),
            pl.BlockSpec((None, 1, tn), lambda j: (layer, 0, j)),
        ],
        out_specs=pl.BlockSpec((rows, tn), lambda j: (0, j)),
        out_shape=jax.ShapeDtypeStruct((rows, n), F32),
        compiler_params=_params("arbitrary"),
        name="adaln",
    )(cond, mod_w, mod_b.reshape(mod_b.shape[0], 1, n))


def _rms_modulate(x, g, shift, scale):
    ms = jnp.mean(x * x, axis=-1, keepdims=True)
    y = x * lax.rsqrt(ms + EPS) * g
    return y * (1.0 + scale) + shift


def _modulate_kernel(*refs, with_pos):
    if with_pos:
        x_ref, pos_ref, g_ref, sh_ref, sc_ref, o_ref = refs
        x = x_ref[0] + pos_ref[...]
    else:
        x_ref, g_ref, sh_ref, sc_ref, o_ref = refs
        x = x_ref[0]
    o_ref[0] = _rms_modulate(x, g_ref[...], sh_ref[0], sc_ref[0]).astype(o_ref.dtype)


def _row_vec_spec(vec, n_batch):
    if vec.shape[0] == 1:
        return pl.BlockSpec((1, 1, D_MODEL), lambda b, i: (0, 0, 0))
    assert vec.shape[0] == n_batch
    return pl.BlockSpec((1, 1, D_MODEL), lambda b, i: (b, 0, 0))


def modulate(x, g, shift, scale, pos=None):
    nb, t, _ = x.shape
    tm = min(t, 512)
    in_specs = [pl.BlockSpec((1, tm, D_MODEL), lambda b, i: (b, i, 0))]
    args = [x]
    if pos is not None:
        in_specs.append(pl.BlockSpec((tm, D_MODEL), lambda b, i: (i, 0)))
        args.append(pos)
    in_specs += [pl.BlockSpec((1, D_MODEL), lambda b, i: (0, 0)), _row_vec_spec(shift, nb), _row_vec_spec(scale, nb)]
    return pl.pallas_call(
        functools.partial(_modulate_kernel, with_pos=pos is not None),
        grid=(nb, t // tm),
        in_specs=in_specs,
        out_specs=pl.BlockSpec((1, tm, D_MODEL), lambda b, i: (b, i, 0)),
        out_shape=jax.ShapeDtypeStruct(x.shape, BF16),
        compiler_params=_params("arbitrary", "arbitrary"),
        name="modulate",
    )(*args, g, shift, scale)


def _modulate_router_kernel(x_ref, g_ref, sh_ref, sc_ref, wr_ref, o_ref, p_ref):
    xm = _rms_modulate(x_ref[0], g_ref[...], sh_ref[0], sc_ref[0])
    xh, xl = _split_bf16(xm)
    wh, wl = _split_bf16(wr_ref[...])
    logits = _dot(xh, wh) + _dot(xl, wh) + _dot(xh, wl)
    lane = lax.broadcasted_iota(jnp.int32, logits.shape, 1)
    logits = jnp.where(lane < N_EXPERTS, logits, -jnp.inf)
    e = jnp.exp(logits - jnp.max(logits, axis=-1, keepdims=True))
    p_ref[0] = e / jnp.sum(e, axis=-1, keepdims=True)
    o_ref[0] = xh


def modulate_router(x, g, shift, scale, w_router_padded):
    nb, t, _ = x.shape
    tm = min(t, 512)
    return pl.pallas_call(
        _modulate_router_kernel,
        grid=(nb, t // tm),
        in_specs=[
            pl.BlockSpec((1, tm, D_MODEL), lambda b, i: (b, i, 0)),
            pl.BlockSpec((1, D_MODEL), lambda b, i: (0, 0)),
            _row_vec_spec(shift, nb),
            _row_vec_spec(scale, nb),
            pl.BlockSpec((D_MODEL, LANES), lambda b, i: (0, 0)),
        ],
        out_specs=[
            pl.BlockSpec((1, tm, D_MODEL), lambda b, i: (b, i, 0)),
            pl.BlockSpec((1, tm, LANES), lambda b, i: (b, i, 0)),
        ],
        out_shape=[
            jax.ShapeDtypeStruct(x.shape, BF16),
            jax.ShapeDtypeStruct((nb, t, LANES), F32),
        ],
        compiler_params=_params("arbitrary", "arbitrary"),
        name="modulate_router",
    )(x, g, shift, scale, w_router_padded)


def _matmul_kernel(*refs, n_a, n_extra, n_out, epilogue, row_scaled):
    a_refs = refs[:n_a]
    w_ref = refs[n_a]
    n_w = 2 if row_scaled else 1
    extra = refs[n_a + n_w:n_a + n_w + n_extra]
    outs = refs[n_a + n_w + n_extra:n_a + n_w + n_extra + n_out]
    wbf_ref = refs[-1]

    @pl.when(pl.program_id(1) == 0)
    def _():
        if row_scaled:
            wbf_ref[...] = (w_ref[...] * refs[n_a + 1][...]).astype(BF16)
        else:
            wbf_ref[...] = w_ref[...].astype(BF16)

    acc = None
    off = 0
    for a_ref in a_refs:
        k = a_ref.shape[-1]
        part = _dot(a_ref[...], wbf_ref[off:off + k, :])
        acc = part if acc is None else acc + part
        off += k
    epilogue(acc, extra, outs, a_refs)


def fused_matmul(a_list, w, layer, col0, ncols, tm, tn, epilogue, extras, extra_specs, out_dtypes, name,
                 w_row_scale=None):
    m = a_list[0].shape[0]
    k_total = w.shape[1]
    assert sum(a.shape[1] for a in a_list) == k_total
    assert m % tm == 0 and ncols % tn == 0 and col0 % tn == 0
    jb = col0 // tn
    in_specs = [pl.BlockSpec((tm, a.shape[1]), lambda j, i: (i, 0)) for a in a_list]
    in_specs.append(pl.BlockSpec((None, k_total, tn), lambda j, i: (layer, 0, jb + j)))
    weights = [w]
    if w_row_scale is not None:
        in_specs.append(pl.BlockSpec((None, k_total, 1), lambda j, i: (layer, 0, 0)))
        weights.append(w_row_scale)
    in_specs.extend(extra_specs)
    out_specs = [pl.BlockSpec((tm, tn), lambda j, i: (i, j)) for _ in out_dtypes]
    out_shape = [jax.ShapeDtypeStruct((m, ncols), dt) for dt in out_dtypes]
    kern = functools.partial(_matmul_kernel, n_a=len(a_list), n_extra=len(extras), n_out=len(out_dtypes),
                             epilogue=epilogue, row_scaled=w_row_scale is not None)
    return pl.pallas_call(
        kern,
        grid=(ncols // tn, m // tm),
        in_specs=in_specs,
        out_specs=out_specs,
        out_shape=out_shape,
        scratch_shapes=[pltpu.VMEM((k_total, tn), BF16)],
        compiler_params=_params("arbitrary", "arbitrary"),
        name=name,
    )(*a_list, *weights, *extras)


def _epi_cast(acc, extra, outs, a_refs):
    outs[0][...] = acc.astype(outs[0].dtype)


def _epi_gelu(acc, extra, outs, a_refs):
    outs[0][...] = (0.5 * acc * (1.0 + lax.erf(acc * (1.0 / math.sqrt(2.0))))).astype(outs[0].dtype)


def _epi_dft(acc, extra, outs, a_refs):
    ch_ref, cl_ref, sh_ref, sl_ref = extra
    a_ref, b_ref = outs
    zh, zl = _split_bf16(acc)
    for g in range(acc.shape[1] // GROUP_DIM):
        sl = slice(g * GROUP_DIM, (g + 1) * GROUP_DIM)
        a_ref[:, sl] = (_dot(zh[:, sl], ch_ref[...]) + _dot(zl[:, sl], ch_ref[...])
                        + _dot(zh[:, sl], cl_ref[...])).astype(a_ref.dtype)
        b_ref[:, sl] = (_dot(zh[:, sl], sh_ref[...]) + _dot(zl[:, sl], sh_ref[...])
                        + _dot(zh[:, sl], sl_ref[...])).astype(b_ref.dtype)


def _epi_residual(acc, extra, outs, a_refs):
    h_ref, gate_ref = extra
    outs[0][...] = h_ref[...] + gate_ref[0] * acc


def _epi_residual_pos(acc, extra, outs, a_refs):
    x_ref, pos_ref, gate_ref = extra
    outs[0][...] = x_ref[...] + pos_ref[...] + gate_ref[0] * acc


def _epi_residual_rownorm(acc, extra, outs, a_refs):
    h_ref, gate_ref = extra
    a = a_refs[0][...].astype(F32)
    inv = lax.rsqrt(jnp.mean(a * a, axis=1, keepdims=True) + EPS)
    outs[0][...] = h_ref[...] + gate_ref[0] * (acc * inv)


def _gate_spec(gate, t, tm, tn):
    per_batch = t // tm
    if gate.shape[0] == 1:
        return pl.BlockSpec((1, 1, tn), lambda j, i: (0, 0, j))
    return pl.BlockSpec((1, 1, tn), lambda j, i: (i // per_batch, 0, j))


def _residual_specs(h_flat, gate, t, tm, tn):
    return [pl.BlockSpec((tm, tn), lambda j, i: (i, j)), _gate_spec(gate, t, tm, tn)]


def _gmlp_kernel(u_ref, v_ref, vg_ref, ws_ref, bs_ref, o_ref):
    for c in range(v_ref.shape[1] // CHUNK):
        rows = slice(c * CHUNK, (c + 1) * CHUNK)
        v = v_ref[0, rows, :].astype(F32)
        vn = (v * lax.rsqrt(jnp.mean(v * v, axis=-1, keepdims=True) + EPS) * vg_ref[...]).astype(BF16)
        for g in range(A_GROUPS):
            sl = slice(g * GROUP_DIM, (g + 1) * GROUP_DIM)
            s = _dot(ws_ref[g].astype(BF16), vn[:, sl]) + bs_ref[:, g:g + 1]
            o_ref[0, rows, sl] = (u_ref[0, rows, sl].astype(F32) * s).astype(o_ref.dtype)


def chunk_gmlp(a, v_g, w_s, b_s_t, layer):
    nb, t, _ = a.shape
    rows = min(t, 512)
    return pl.pallas_call(
        _gmlp_kernel,
        grid=(nb, t // rows),
        in_specs=[
            pl.BlockSpec((1, rows, A_WIDTH), lambda b, c: (b, c, 0)),
            pl.BlockSpec((1, rows, A_WIDTH), lambda b, c: (b, c, 1)),
            pl.BlockSpec((None, 1, A_WIDTH), lambda b, c: (layer, 0, 0)),
            pl.BlockSpec((None, A_GROUPS, CHUNK, CHUNK), lambda b, c: (layer, 0, 0, 0)),
            pl.BlockSpec((None, CHUNK, A_GROUPS), lambda b, c: (layer, 0, 0)),
        ],
        out_specs=pl.BlockSpec((1, rows, A_WIDTH), lambda b, c: (b, c, 0)),
        out_shape=jax.ShapeDtypeStruct((nb, t, A_WIDTH), BF16),
        compiler_params=_params("arbitrary", "arbitrary"),
        name="chunk_gmlp",
    )(a, a, v_g.reshape(v_g.shape[0], 1, A_WIDTH), w_s, b_s_t)


def _dft_tables(n):
    k = np.arange(n, dtype=np.int64)
    ang = 2.0 * np.pi * ((k[:, None] * k[None, :]) % n).astype(np.float64) / n
    return np.cos(ang), np.sin(ang)


def _hi_lo(table):
    hi = jnp.asarray(table, dtype=F32).astype(BF16)
    lo = (jnp.asarray(table, dtype=F32) - hi.astype(F32)).astype(BF16)
    return hi, lo


def _token_dft_kernel(ct_ref, st_ref, a_ref, b_ref, o_ref, *, scale):
    y = _dot(ct_ref[...], a_ref[0]) - _dot(st_ref[...], b_ref[0])
    o_ref[0] = (y * scale).astype(o_ref.dtype)


def token_dft(a, b):
    nb, t, w = a.shape
    cos_t, sin_t = _dft_tables(t)
    ct = jnp.asarray(cos_t, dtype=F32).astype(BF16)
    st = jnp.asarray(sin_t, dtype=F32).astype(BF16)
    tm = min(t, 512)
    kern = functools.partial(_token_dft_kernel, scale=1.0 / math.sqrt(t * GROUP_DIM))
    return pl.pallas_call(
        kern,
        grid=(nb, t // tm),
        in_specs=[
            pl.BlockSpec((tm, t), lambda b_, i: (i, 0)),
            pl.BlockSpec((tm, t), lambda b_, i: (i, 0)),
            pl.BlockSpec((1, t, w), lambda b_, i: (b_, 0, 0)),
            pl.BlockSpec((1, t, w), lambda b_, i: (b_, 0, 0)),
        ],
        out_specs=pl.BlockSpec((1, tm, w), lambda b_, i: (b_, i, 0)),
        out_shape=jax.ShapeDtypeStruct((nb, t, w), BF16),
        compiler_params=_params("arbitrary", "arbitrary"),
        name="token_dft",
    )(ct, st, a, b)


def mixer_ab(xn, h, gate, ab_w_in, ab_w_out, gm_v_g, gm_w_s, gm_b_s_t, layer, pos=None):
    nb, t, _ = xn.shape
    m = nb * t
    xf = xn.reshape(m, D_MODEL)
    tm = min(m, 512)
    (a,) = fused_matmul([xf], ab_w_in, layer, 0, 2 * A_WIDTH, tm, 1024, _epi_gelu, [], [], [BF16], "ab_in_gelu")
    c128, s128 = _dft_tables(GROUP_DIM)
    tables = [*_hi_lo(c128), *_hi_lo(s128)]
    table_specs = [pl.BlockSpec((GROUP_DIM, GROUP_DIM), lambda j, i: (0, 0)) for _ in tables]
    za, zb = fused_matmul([xf], ab_w_in, layer, 2 * A_WIDTH, B_WIDTH, tm, 1024, _epi_dft, tables, table_specs,
                          [BF16, BF16], "ab_in_dft")
    ya = chunk_gmlp(a.reshape(nb, t, 2 * A_WIDTH), gm_v_g, gm_w_s, gm_b_s_t, layer)
    yb = token_dft(za.reshape(nb, t, B_WIDTH), zb.reshape(nb, t, B_WIDTH))
    hf = h.reshape(m, D_MODEL)
    tn = 1024
    if pos is None:
        epi, extras, specs = _epi_residual, [hf, gate], _residual_specs(hf, gate, t, tm, tn)
    else:
        per_batch = t // tm
        epi, extras = _epi_residual_pos, [hf, pos, gate]
        specs = [pl.BlockSpec((tm, tn), lambda j, i: (i, j)), pl.BlockSpec((tm, tn), lambda j, i: (i % per_batch, j)),
                 _gate_spec(gate, t, tm, tn)]
    (out,) = fused_matmul([ya.reshape(m, A_WIDTH), yb.reshape(m, B_WIDTH)], ab_w_out, layer, 0, D_MODEL, tm, tn,
                          epi, extras, specs, [F32], "ab_out")
    return out.reshape(nb, t, D_MODEL)


def _conv_kernel(prev_ref, main_ref, next_ref, w_ref, b_ref, o_ref, *, rows):
    r = pl.program_id(1)
    last = pl.num_programs(1) - 1
    prev = jnp.where(r > 0, prev_ref[0].astype(F32), 0.0)
    nxt = jnp.where(r < last, next_ref[0].astype(F32), 0.0)
    full = jnp.concatenate([prev, main_ref[0].astype(F32), nxt], axis=0)
    n = full.shape[0]
    w = w_ref[...]
    acc = None
    for k in range(D_CONV):
        shift = (D_CONV // 2 - k) % n
        term = (full if shift == 0 else pltpu.roll(full, shift, 0)) * w[k:k + 1, :]
        acc = term if acc is None else acc + term
    y = acc[BF16_SUBLANES:BF16_SUBLANES + rows, :] + b_ref[...]
    o_ref[0] = _silu(y).astype(o_ref.dtype)


def dwconv_silu(z, conv_w, conv_b, layer):
    nb, t, ch = z.shape
    rows = min(t, 512)
    tc = 1024
    halo = BF16_SUBLANES
    per = rows // halo
    n_halo = t // halo
    kern = functools.partial(_conv_kernel, rows=rows)
    return pl.pallas_call(
        kern,
        grid=(nb, t // rows, ch // tc),
        in_specs=[
            pl.BlockSpec((1, halo, tc), lambda b, r, c: (b, jnp.maximum(r * per - 1, 0), c)),
            pl.BlockSpec((1, rows, tc), lambda b, r, c: (b, r, c)),
            pl.BlockSpec((1, halo, tc), lambda b, r, c: (b, jnp.minimum((r + 1) * per, n_halo - 1), c)),
            pl.BlockSpec((None, D_CONV, tc), lambda b, r, c: (layer, 0, c)),
            pl.BlockSpec((None, 1, tc), lambda b, r, c: (layer, 0, c)),
        ],
        out_specs=pl.BlockSpec((1, rows, tc), lambda b, r, c: (b, r, c)),
        out_shape=jax.ShapeDtypeStruct(z.shape, BF16),
        compiler_params=_params("arbitrary", "arbitrary", "arbitrary"),
        name="dwconv_silu",
    )(z, z, z, conv_w, conv_b.reshape(conv_b.shape[0], 1, conv_b.shape[1]))


def _softplus(x):
    return jnp.maximum(x, 0.0) + jnp.log(1.0 + jnp.exp(-jnp.abs(x)))


def _prefix_sum(x, axis):
    n = x.shape[axis]
    idx = lax.broadcasted_iota(jnp.int32, x.shape, axis)
    s = 1
    while s < n:
        x = x + jnp.where(idx >= s, pltpu.roll(x, s, axis), 0.0)
        s *= 2
    return x


def _ssd_prep_kernel(raw_ref, bias_ref, a_ref, cumc_ref, cumr_ref, dtr_ref, wr_ref):
    L = SSD_CHUNK
    gw = 2 * HEADS_PER_GROUP
    lane = lax.broadcasted_iota(jnp.int32, (L, raw_ref.shape[2]), 1)
    is_bwd = (lane % gw) >= HEADS_PER_GROUP

    def chunk(c, carry):
        rows = pl.ds(pl.multiple_of(c * L, L), L)
        dt = _softplus(raw_ref[0, rows, :] + bias_ref[...])
        la = dt * (a_ref[...] * LOG2E)
        cum = _prefix_sum(la, 0)
        tot = cum[L - 1:L, :]
        cum = jnp.where(is_bwd, tot - cum + la, cum)
        w = jnp.exp2(tot - cum) * dt
        cum_t = cum.T
        dt_t = dt.T
        w_t = w.T
        for g in range(N_BC_GROUPS):
            ls = slice(g * gw, (g + 1) * gw)
            cumc_ref[0, g, rows, :] = cum[:, ls]
            cumr_ref[0, g, :, rows] = cum_t[ls, :]
            dtr_ref[0, g, :, rows] = dt_t[ls, :]
            wr_ref[0, g, :, rows] = w_t[ls, :]
        return carry

    lax.fori_loop(0, raw_ref.shape[1] // L, chunk, 0)


def ssd_prep(dt_raw, bias, a):
    nb, t, n = dt_raw.shape
    g, gw = N_BC_GROUPS, 2 * HEADS_PER_GROUP
    col = jax.ShapeDtypeStruct((nb, g, t, gw), F32)
    row = jax.ShapeDtypeStruct((nb, g, gw, t), F32)
    col_spec = pl.BlockSpec((1, g, t, gw), lambda b: (b, 0, 0, 0))
    row_spec = pl.BlockSpec((1, g, gw, t), lambda b: (b, 0, 0, 0))
    return pl.pallas_call(
        _ssd_prep_kernel,
        grid=(nb,),
        in_specs=[
            pl.BlockSpec((1, t, n), lambda b: (b, 0, 0)),
            pl.BlockSpec((1, n), lambda b: (0, 0)),
            pl.BlockSpec((1, n), lambda b: (0, 0)),
        ],
        out_specs=[col_spec, row_spec, row_spec, row_spec],
        out_shape=[col, row, row, row],
        compiler_params=_params("arbitrary"),
        name="ssd_prep",
    )(dt_raw, bias, a)


def _ssd_chunk(c, direction, with_y, x_ref, b_ref, c_ref, cumc_ref, cumr_ref, dtr_ref, wr_ref, y_ref, state_ref):
    L = SSD_CHUNK
    hp = HEADS_PER_GROUP
    h0 = direction * hp
    rows = pl.ds(pl.multiple_of(c * L, L), L)
    lane_lo = lax.broadcasted_iota(jnp.int32, (L, 2 * HEADDIM), 1) < HEADDIM
    row_lo = lane_lo[:1]
    edge = L - 1 if direction == 0 else 0

    bm = b_ref[0, rows, :]
    bm_t = bm.astype(F32).T
    cumr = cumr_ref[0, 0, :, rows]
    wr = wr_ref[0, 0, :, rows]

    if with_y:
        cm = c_ref[0, rows, :]
        cb = lax.dot_general(cm, bm, (((1,), (1,)), ((), ())), preferred_element_type=F32)
        ii = lax.broadcasted_iota(jnp.int32, (L, L), 0)
        jj = lax.broadcasted_iota(jnp.int32, (L, L), 1)
        causal = (jj <= ii) if direction == 0 else (jj >= ii)
        cbm = jnp.where(causal, cb, 0.0).astype(BF16)
        cumc = cumc_ref[0, 0, rows, :]
        dtr = dtr_ref[0, 0, :, rows].astype(BF16)

    for p in range(hp // 2):
        k1, k2 = h0 + 2 * p, h0 + 2 * p + 1
        cols = slice(p * 2 * HEADDIM, (p + 1) * 2 * HEADDIM)
        s_prev = state_ref[direction, :, cols]
        xp = x_ref[0, rows, cols]
        zero = jnp.zeros_like(xp)
        rhs2 = jnp.concatenate([jnp.where(lane_lo, xp, zero), jnp.where(lane_lo, zero, xp)], axis=0)
        if with_y:
            ms, cum_b = [], []
            for k in (k1, k2):
                cum_b.append(jnp.broadcast_to(cumc[:, k:k + 1], (L, L)))
                seg = jnp.minimum(cum_b[-1] - cumr[k:k + 1, :], 0.0)
                ms.append(cbm * jnp.exp2(seg).astype(BF16) * dtr[k:k + 1, :])
            edec = jnp.exp2(jnp.where(lane_lo, cum_b[0], cum_b[1]))
            y_ref[direction, rows, cols] = (_dot(jnp.concatenate(ms, axis=1), rhs2)
                                            + _dot(cm, s_prev.astype(BF16)) * edec)
        bw = jnp.concatenate([(bm_t * wr[k1:k1 + 1, :]).astype(BF16), (bm_t * wr[k2:k2 + 1, :]).astype(BF16)], axis=1)
        d1 = jnp.exp2(cumr[k1:k1 + 1, edge:edge + 1])
        d2 = jnp.exp2(cumr[k2:k2 + 1, edge:edge + 1])
        dec = jnp.where(row_lo, jnp.broadcast_to(d1, row_lo.shape), jnp.broadcast_to(d2, row_lo.shape))
        state_ref[direction, :, cols] = s_prev * dec + _dot(bw, rhs2)


def _ssd_kernel(*refs, n_chunks, with_y):
    if with_y:
        (x_ref, b_ref, c_ref, z_ref, cumc_ref, cumr_ref, dtr_ref, wr_ref, dskip_ref, s0_ref,
         y_out_ref, sfin_ref, state_ref, y_ref) = refs
        state_ref[...] = s0_ref[:, 0, 0]
    else:
        x_ref, b_ref, cumr_ref, wr_ref, sfin_ref, state_ref = refs
        c_ref = z_ref = cumc_ref = dtr_ref = y_ref = None
        state_ref[...] = jnp.zeros_like(state_ref)
    args =(x_ref, b_ref, c_ref, cumc_ref, cumr_ref, dtr_ref, wr_ref, y_ref, state_ref)

    def both(c, carry):
        _ssd_chunk(c, 0, with_y, *args)
        _ssd_chunk(n_chunks - 1 - c, 1, with_y, *args)
        return carry

    lax.fori_loop(0, n_chunks, both, 0, unroll=min(8, n_chunks))
    sfin_ref[:, 0, 0] = state_ref[...]

    if with_y:
        def gate(c, carry):
            rows = pl.ds(pl.multiple_of(c * SSD_CHUNK, SSD_CHUNK), SSD_CHUNK)
            y = y_ref[0, rows, :] + y_ref[1, rows, :] + x_ref[0, rows, :].astype(F32) * dskip_ref[0]
            y_out_ref[0, rows, :] = (y * _silu(z_ref[0, rows, :].astype(F32))).astype(y_out_ref.dtype)
            return carry

        lax.fori_loop(0, n_chunks, gate, 0)


def ssd_scan(xbc, prep, d_skip=None, s0=None, z=None):
    nb, t, _ = xbc.shape
    g, gw = N_BC_GROUPS, 2 * HEADS_PER_GROUP
    cumc, cumr, dtr, wr = prep
    with_y = z is not None
    b_off = D_INNER // D_STATE
    c_off = (D_INNER + GN) // D_STATE
    x_spec = pl.BlockSpec((1, t, GROUP_WIDTH), lambda b, gi: (b, 0, gi))
    b_spec = pl.BlockSpec((1, t, D_STATE), lambda b, gi: (b, 0, b_off + gi))
    c_spec = pl.BlockSpec((1, t, D_STATE), lambda b, gi: (b, 0, c_off + gi))
    col_spec = pl.BlockSpec((1, 1, t, gw), lambda b, gi: (b, gi, 0, 0))
    row_spec = pl.BlockSpec((1, 1, gw, t), lambda b, gi: (b, gi, 0, 0))
    s_spec = pl.BlockSpec((2, 1, 1, D_STATE, GROUP_WIDTH), lambda b, gi: (0, b, gi, 0, 0))
    s_shape = jax.ShapeDtypeStruct((2, nb, g, D_STATE, GROUP_WIDTH), F32)
    state = pltpu.VMEM((2, D_STATE, GROUP_WIDTH), F32)
    kern = functools.partial(_ssd_kernel, n_chunks=t // SSD_CHUNK, with_y=with_y)
    if not with_y:
        assert s0 is None and d_skip is None
        return pl.pallas_call(
            kern,
            grid=(nb, g),
            in_specs=[x_spec, b_spec, row_spec, row_spec],
            out_specs=s_spec,
            out_shape=s_shape,
            scratch_shapes=[state],
            compiler_params=_params("arbitrary", "arbitrary"),
            name="ssd_states",
        )(xbc, xbc, cumr, wr)
    dskip = jnp.repeat(d_skip, HEADDIM).reshape(g, 1, GROUP_WIDTH)
    return pl.pallas_call(
        kern,
        grid=(nb, g),
        in_specs=[x_spec, b_spec, c_spec, x_spec, col_spec, row_spec, row_spec, row_spec,
                  pl.BlockSpec((1, 1, GROUP_WIDTH), lambda b, gi: (gi, 0, 0)), s_spec],
        out_specs=[x_spec, s_spec],
        out_shape=[jax.ShapeDtypeStruct((nb, t, D_INNER), BF16), s_shape],
        scratch_shapes=[state, pltpu.VMEM((2, t, GROUP_WIDTH), F32)],
        compiler_params=_params("arbitrary", "arbitrary"),
        name="ssd_scan",
    )(xbc, xbc, xbc, z, cumc, cumr, dtr, wr, dskip, s0)


def _group_major(v):
    return v.reshape(2, N_BC_GROUPS, HEADS_PER_GROUP).transpose(1, 0, 2).reshape(1, 2 * N_HEADS)


def ssd_project(xn, w_in, w_dt, conv_w, conv_b, dt_bias, a, layer, with_z):
    nb, t, _ = xn.shape
    m = nb * t
    xf = xn.reshape(m, D_MODEL)
    tm = min(m, 512)
    conv_cols = CONV_DIM if with_z else D_INNER + GN
    (xbc,) = fused_matmul([xf], w_in, layer, D_INNER, conv_cols, tm, 1024, _epi_cast, [], [], [BF16], "ssd_in_xbc")
    (dt_raw,) = fused_matmul([xf], w_dt, 0, 0, 2 * N_HEADS, tm, 2 * N_HEADS, _epi_cast, [], [], [F32], "ssd_in_dt")
    xbc = dwconv_silu(xbc.reshape(nb, t, conv_cols), conv_w, conv_b, layer)
    prep = ssd_prep(dt_raw.reshape(nb, t, 2 * N_HEADS), _group_major(dt_bias), _group_major(a))
    z = None
    if with_z:
        (z,) = fused_matmul([xf], w_in, layer, 0, D_INNER, tm, 1024, _epi_cast, [], [], [BF16], "ssd_in_z")
        z = z.reshape(nb, t, D_INNER)
    return xbc, prep, z


def ssd_mix(xn, xc, h, gate, w_in, conv_w, conv_b, dt_bias, a_log, d_skip, norm_g, w_out, layer):
    nb, t, _ = xn.shape
    m = nb * t
    a = -jnp.exp(a_log[layer].astype(F32))
    w_dt = w_in[layer, :, D_INNER + CONV_DIM:].reshape(D_MODEL, 2, N_BC_GROUPS, HEADS_PER_GROUP)
    w_dt = w_dt.transpose(0, 2, 1, 3).reshape(1, D_MODEL, 2 * N_HEADS)
    xbc_c, prep_c, _ = ssd_project(xc, w_in, w_dt, conv_w, conv_b, dt_bias[layer], a, layer, False)
    s_ctx = ssd_scan(xbc_c, prep_c)
    xbc, prep, z = ssd_project(xn, w_in, w_dt, conv_w, conv_b, dt_bias[layer], a, layer, True)
    y, _ = ssd_scan(xbc, prep, d_skip[layer], s_ctx, z)
    hf = h.reshape(m, D_MODEL)
    tm, tn = 512, 512
    (out,) = fused_matmul([y.reshape(m, D_INNER)], w_out, layer, 0, D_MODEL, tm, tn, _epi_residual_rownorm,
                          [hf, gate], _residual_specs(hf, gate, t, tm, tn), [F32], "ssd_out",
                          w_row_scale=norm_g.reshape(norm_g.shape[0], D_INNER, 1))
    return out.reshape(nb, t, D_MODEL)


def _route_kernel(p_ref, tri_ref, pos_ref, post_ref, gate_ref, *, cap):
    probs = p_ref[0]
    t = probs.shape[0]
    pe = probs.T[:N_EXPERTS, :]
    bits = pltpu.bitcast(pe, jnp.int32)

    def step(i, v):
        cand = v | jnp.left_shift(jnp.int32(1), 30 - i)
        cnt = jnp.sum((bits >= cand).astype(F32), axis=1, keepdims=True)
        return jnp.where(cnt >= cap, cand, v)

    thr = lax.fori_loop(0, 31, step, jnp.zeros((N_EXPERTS, 1), jnp.int32))
    gt = bits > thr
    eq = bits == thr
    need = cap - jnp.sum(gt.astype(F32), axis=1, keepdims=True)
    tri = tri_ref[...]
    eq_rank = _dot(eq.astype(BF16), tri)
    sel = gt | (eq & (eq_rank <= need))
    pos = _dot(sel.astype(BF16), tri) - 1.0
    pos = jnp.where(sel, pos, -1.0)
    pos_ref[0] = pos
    pos_full = jnp.concatenate([pos, jnp.full((LANES - N_EXPERTS, t), -1.0, F32)], axis=0)
    pos_t = pos_full.T
    post_ref[0] = pos_t
    gate = jnp.where(pos_t >= 0.0, probs, 0.0)
    hi = gate.astype(BF16).astype(F32)
    mid = (gate - hi).astype(BF16).astype(F32)
    lo = (gate - hi - mid).astype(BF16).astype(F32)
    gate_ref[0] = (hi + pltpu.roll(mid, N_EXPERTS, 1) + pltpu.roll(lo, 2 * N_EXPERTS, 1)).astype(BF16)


def route(probs, cap):
    nb, t, _ = probs.shape
    idx = np.arange(t)
    tri = jnp.asarray((idx[:, None] <= idx[None, :]).astype(np.float32), dtype=BF16)
    kern = functools.partial(_route_kernel, cap=cap)
    return pl.pallas_call(
        kern,
        grid=(nb,),
        in_specs=[
            pl.BlockSpec((1, t, LANES), lambda b: (b, 0, 0)),
            pl.BlockSpec((t, t), lambda b: (0, 0)),
        ],
        out_specs=[
            pl.BlockSpec((1, N_EXPERTS, t), lambda b: (b, 0, 0)),
            pl.BlockSpec((1, t, LANES), lambda b: (b, 0, 0)),
            pl.BlockSpec((1, t, LANES), lambda b: (b, 0, 0)),
        ],
        out_shape=[
            jax.ShapeDtypeStruct((nb, N_EXPERTS, t), F32),
            jax.ShapeDtypeStruct((nb, t, LANES), F32),
            jax.ShapeDtypeStruct((nb, t, LANES), BF16),
        ],
        compiler_params=_params("arbitrary"),
        name="route",
    )(probs, tri)


def _gather_kernel(pos_ref, x_ref, gate_ref, *rest, cap, n_batch):
    o_ref, gs_ref = rest[-2:]
    b = pl.program_id(0)
    e = pl.program_id(1)

    @pl.when(b < n_batch)
    def _():
        pos = pos_ref[0, 0]
        t = pos.shape[1]
        slot = lax.broadcasted_iota(jnp.int32, (cap, t), 0).astype(F32)
        onehot = (slot == pos).astype(BF16)
        o_ref[0] = _dot(onehot, x_ref[0]).astype(o_ref.dtype)
        pieces = _dot(onehot, gate_ref[0])
        lane = lax.broadcasted_iota(jnp.int32, pieces.shape, 1)
        mine = (lane < 3 * N_EXPERTS) & (lane % N_EXPERTS == e)
        gs_ref[0] = jnp.sum(jnp.where(mine, pieces, 0.0), axis=1, keepdims=True)

    @pl.when(b >= n_batch)
    def _():
        o_ref[...] = jnp.zeros_like(o_ref)
        gs_ref[...] = jnp.zeros_like(gs_ref)


def gather_tokens(pos, xm, gate_split, cap, total_rows=None, row0=0, dest=None):
    nb, t, _ = xm.shape
    total_rows = nb * cap if total_rows is None else total_rows
    assert row0 % cap == 0
    blk0 = row0 // cap
    tail = 0 if dest is not None else total_rows - row0 - nb * cap
    assert tail % cap == 0
    kern = functools.partial(_gather_kernel, cap=cap, n_batch=nb)
    last = nb - 1
    in_specs = [
        pl.BlockSpec((1, 1, 1, t), lambda b, e: (jnp.minimum(b, last), e, 0, 0)),
        pl.BlockSpec((1, t, D_MODEL), lambda b, e: (jnp.minimum(b, last), 0, 0)),
        pl.BlockSpec((1, t, LANES), lambda b, e: (jnp.minimum(b, last), 0, 0)),
    ]
    args = [pos.reshape(nb, N_EXPERTS, 1, t), xm, gate_split]
    aliases = {}
    if dest is not None:
        in_specs += [pl.BlockSpec(memory_space=pl.ANY), pl.BlockSpec(memory_space=pl.ANY)]
        args += list(dest)
        aliases = {3: 0, 4: 1}
    return pl.pallas_call(
        kern,
        grid=(nb + tail // cap, N_EXPERTS),
        in_specs=in_specs,
        out_specs=[
            pl.BlockSpec((1, cap, D_MODEL), lambda b, e: (e, blk0 + b, 0)),
            pl.BlockSpec((1, cap, 1), lambda b, e: (e, blk0 + b, 0)),
        ],
        out_shape=[
            jax.ShapeDtypeStruct((N_EXPERTS, total_rows, D_MODEL), BF16),
            jax.ShapeDtypeStruct((N_EXPERTS, total_rows, 1), F32),
        ],
        input_output_aliases=aliases,
        compiler_params=_params("arbitrary", "arbitrary"),
        name="gather_tokens",
    )(*args)


def _ffn_kernel(x_ref, gs_ref, wg_ref, wu_ref, wd_ref, o_ref, acc_ref):
    f = pl.program_id(2)
    x = x_ref[0]
    gate = _dot(x, wg_ref[...].astype(BF16))
    up = _dot(x, wu_ref[...].astype(BF16))
    hid = (_silu(gate) * up).astype(BF16)

    @pl.when(f == 0)
    def _():
        acc_ref[...] = jnp.zeros_like(acc_ref)

    acc_ref[...] += _dot(hid, wd_ref[...].astype(BF16))

    @pl.when(f == pl.num_programs(2) - 1)
    def _():
        o_ref[0] = (acc_ref[...] * gs_ref[0]).astype(o_ref.dtype)


def expert_ffn(xe, gate_slot, w_gate, w_up, w_down, layer):
    _, m, _ = xe.shape
    tm = m if m <= 1024 else m // 2
    assert m % tm == 0 and tm % BF16_SUBLANES == 0
    tf = 256
    return pl.pallas_call(
        _ffn_kernel,
        grid=(N_EXPERTS, m // tm, D_MODEL // tf),
        in_specs=[
            pl.BlockSpec((1, tm, D_MODEL), lambda e, i, f: (e, i, 0)),
            pl.BlockSpec((1, tm, 1), lambda e, i, f: (e, i, 0)),
            pl.BlockSpec((None, None, D_MODEL, tf), lambda e, i, f: (layer, e, 0, f)),
            pl.BlockSpec((None, None, D_MODEL, tf), lambda e, i, f: (layer, e, 0, f)),
            pl.BlockSpec((None, None, tf, D_MODEL), lambda e, i, f: (layer, e, f, 0)),
        ],
        out_specs=pl.BlockSpec((1, tm, D_MODEL), lambda e, i, f: (e, i, 0)),
        out_shape=jax.ShapeDtypeStruct(xe.shape, BF16),
        scratch_shapes=[pltpu.VMEM((tm, D_MODEL), F32)],
        compiler_params=_params("arbitrary", "arbitrary", "arbitrary"),
        name="expert_ffn",
    )(xe, gate_slot, w_gate, w_up, w_down)


def _combine_kernel(*refs, cap, final_norm):
    post_ref, y_ref, h_ref, g2_ref = refs[:4]
    o_ref = refs[-1]
    tq = post_ref.shape[1]
    slot = lax.broadcasted_iota(jnp.int32, (tq, cap), 1).astype(F32)
    pos_t = post_ref[0]
    acc = None
    for e in range(N_EXPERTS):
        onehot = (slot == pos_t[:, e:e + 1]).astype(BF16)
        part = _dot(onehot, y_ref[e])
        acc = part if acc is None else acc + part
    out = h_ref[0] + g2_ref[0] * acc
    if final_norm:
        out = out * lax.rsqrt(jnp.mean(out * out, axis=-1, keepdims=True) + EPS) * refs[4][...]
    o_ref[0] = out


def combine(pos_t, ye, h, g2, cap, final_g=None, row0=0):
    nb, t, _ = h.shape
    tq = min(t, 256)
    assert row0 % cap == 0
    blk0 = row0 // cap
    if g2.shape[0] == 1:
        g2_spec = pl.BlockSpec((1, 1, D_MODEL), lambda b, i: (0, 0, 0))
    else:
        g2_spec = pl.BlockSpec((1, 1, D_MODEL), lambda b, i: (b, 0, 0))
    in_specs = [
        pl.BlockSpec((1, tq, LANES), lambda b, i: (b, i, 0)),
        pl.BlockSpec((N_EXPERTS, cap, D_MODEL), lambda b, i: (0, blk0 + b, 0)),
        pl.BlockSpec((1, tq, D_MODEL), lambda b, i: (b, i, 0)),
        g2_spec,
    ]
    args = [pos_t, ye, h, g2]
    if final_g is not None:
        in_specs.append(pl.BlockSpec((1, D_MODEL), lambda b, i: (0, 0)))
        args.append(final_g)
    kern = functools.partial(_combine_kernel, cap=cap, final_norm=final_g is not None)
    return pl.pallas_call(
        kern,
        grid=(nb, t // tq),
        in_specs=in_specs,
        out_specs=pl.BlockSpec((1, tq, D_MODEL), lambda b, i: (b, i, 0)),
        out_shape=jax.ShapeDtypeStruct(h.shape, F32),
        compiler_params=_params("arbitrary", "arbitrary"),
        name="combine",
    )(*args)


def expert_choice_ffn(streams, g, w_router, w_gate, w_up, w_down, layer, final_g=None):
    wr = jnp.pad(w_router[layer], ((0, 0), (0, LANES - N_EXPERTS)))
    caps = [CAPACITY_FACTOR * h.shape[1] // N_EXPERTS for h, _, _, _ in streams]
    rows = [h.shape[0] * cap for (h, _, _, _), cap in zip(streams, caps)]
    row0 = [sum(rows[:k]) for k in range(len(rows))]
    routed, dest = [], None
    for (h, shift, scale, _), cap, r0 in zip(streams, caps, row0):
        xm, probs = modulate_router(h, g, shift, scale, wr)
        pos, pos_t, gate_split = route(probs, cap)
        dest = gather_tokens(pos, xm, gate_split, cap, sum(rows), r0, dest)
        routed.append(pos_t)
    ye = expert_ffn(*dest, w_gate, w_up, w_down, layer)
    return [combine(pos_t, ye, h, g2, cap, final_g if k == 0 else None, r0)
            for k, ((h, _, _, g2), pos_t, cap, r0) in enumerate(zip(streams, routed, caps, row0))]


def _sincos_2d(rows, cols, dim):
    quarter = dim // 4
    omega = 1.0 / (10000.0 ** (np.arange(quarter, dtype=np.float64) / quarter))
    r = np.repeat(np.arange(rows, dtype=np.float64), cols)[:, None] * omega
    cl = np.tile(np.arange(cols, dtype=np.float64), rows)[:, None] * omega
    return jnp.asarray(np.concatenate([np.sin(r), np.cos(r), np.sin(cl), np.cos(cl)], axis=-1), dtype=F32)


def kernel(x, c, ctx, c_ctx, mod_w, mod_b, norm_g, final_g, ab_w_in, ab_w_out, gm_v_g, gm_w_s, gm_b_s,
           ssd_w_in, ssd_conv_w, ssd_conv_b, ssd_dt_bias, ssd_a_log, ssd_d, ssd_norm_g, ssd_w_out,
           moe_w_router, moe_w_gate, moe_w_up, moe_w_down):
    nb, n, _ = x.shape
    depth = mod_w.shape[0]
    pos = _sincos_2d(n // GRID_W, GRID_W, D_MODEL)
    h = x
    hc = ctx
    cond = jnp.concatenate([c, c_ctx[None], jnp.zeros((16 - nb - 1, D_MODEL), F32)], axis=0)
    gm_b_s_t = jnp.swapaxes(gm_b_s, 1, 2)
    for i in range(depth):
        last = i == depth - 1
        even = i % 2 == 0
        j = i // 2
        ctx_reaches_latent = (not last) or (not even)
        mod = adaln(cond, mod_w, mod_b, i).reshape(16, 6, 1, D_MODEL)
        sh1, sc1, g1, sh2, sc2, g2 = (mod[:nb, k] for k in range(6))
        csh1, csc1, cg1, csh2, csc2, cg2 = (mod[nb:nb + 1, k] for k in range(6))
        g_mix = norm_g[i, 0][None]
        g_ffn = norm_g[i, 1][None]
        if pos is not None and not even:
            h, pos = h + pos[None], None
        xn = modulate(h, g_mix, sh1, sc1, pos)
        if ctx_reaches_latent:
            xc = modulate(hc, g_mix, csh1, csc1)
        if even:
            h = mixer_ab(xn, h, g1, ab_w_in, ab_w_out, gm_v_g, gm_w_s, gm_b_s_t, j, pos)
            pos = None
            if not last:
                hc = mixer_ab(xc, hc, cg1, ab_w_in, ab_w_out, gm_v_g, gm_w_s, gm_b_s_t, j)
        else:
            assert last, "context outputs of a state-space layer are only needed before the last layer"
            h = ssd_mix(xn, xc, h, g1, ssd_w_in, ssd_conv_w, ssd_conv_b, ssd_dt_bias, ssd_a_log, ssd_d,
                        ssd_norm_g, ssd_w_out, j)
        streams = [(h, sh2, sc2, g2)]
        if not last:
            streams.append((hc, csh2, csc2, cg2))
        outs = expert_choice_ffn(streams, g_ffn, moe_w_router, moe_w_gate, moe_w_up, moe_w_down, i,
                                 final_g[None] if last else None)
        h = outs[0]
        if not last:
            hc = outs[1]
    return h
```
